```python
import math
import jax, jax.numpy as jnp
from jax import lax
import numpy as np

D_MODEL = 1024
BATCH = 4
SEQ = 8192
DEPTH = 2
DEC_BATCH = 128
DEC_SEQ = 1
PAST_LEN = 16384
PAGE_SIZE = 128

HEAD_DIM = 64
DSA_HEADS = 8
DSA_IDX_HEADS = 8
DSA_IDX_DIM = 64
DSA_TOPK = 256
DSA_QB = 128
MOBA_HEADS = 8
MOBA_KV_HEADS = 2
MOBA_BLOCK = 256
MOBA_TOPK = 3
MOBA_QB = 32
MLA_HEADS = 8
MLA_Q_LORA = 256
MLA_KV_LORA = 128
MLA_NOPE = 64
MLA_ROPE = 32
MLA_V = 64
MLA_QB = 128
ROPE_THETA = 10000.0
N_BRANCH = 3
BRANCH_WIDTH = 512
REL_BUCKETS = 32
REL_MAX_DIST = 128
PEER_HEADS = 8
PEER_NKEYS = 128
PEER_EXPERTS = PEER_NKEYS * PEER_NKEYS
PEER_KEY_DIM = 256
PEER_TOPK = 16
PEER_BLOCK = 128
DN_ALPHA = (2 * DEPTH) ** 0.25
DN_BETA = (8 * DEPTH) ** -0.25
LN_EPS = 1e-5
RMS_EPS = 1e-6
F32 = jnp.float32

IN_WIDTHS = (DSA_HEADS * HEAD_DIM, HEAD_DIM, HEAD_DIM, DSA_IDX_HEADS * DSA_IDX_DIM, DSA_IDX_DIM,
             DSA_IDX_HEADS, MOBA_HEADS * HEAD_DIM, MOBA_KV_HEADS * HEAD_DIM, MOBA_KV_HEADS * HEAD_DIM,
             MLA_Q_LORA, MLA_KV_LORA, MLA_ROPE, N_BRANCH * D_MODEL)
IN_VALUE_GROUPS = (2, 8)

kernel_name = 'hybrid_dsa_moba_mla_peer_decoder_step'


def _layer_norm(x, g, b):
    xf = x.astype(F32)
    mu = jnp.mean(xf, -1, keepdims=True)
    var = jnp.mean(jnp.square(xf - mu), -1, keepdims=True)
    return ((xf - mu) * lax.rsqrt(var + LN_EPS) * g.astype(F32) + b.astype(F32)).astype(x.dtype)


def _rms_norm(x, g):
    xf = x.astype(F32)
    return (xf * lax.rsqrt(jnp.mean(xf * xf, -1, keepdims=True) + RMS_EPS) * g.astype(F32)).astype(x.dtype)


def _rope(x, pos):
    half = x.shape[-1] // 2
    freq = ROPE_THETA ** (-jnp.arange(half, dtype=F32) / half)
    ang = pos.astype(F32)[..., None] * freq
    ang = ang.reshape(ang.shape[:2] + (1,) * (x.ndim - 3) + (half,))
    cos, sin = jnp.cos(ang), jnp.sin(ang)
    xf = x.astype(F32)
    x1, x2 = xf[..., :half], xf[..., half:]
    return jnp.concatenate([x1 * cos - x2 * sin, x2 * cos + x1 * sin], -1).astype(x.dtype)


def _rel_bucket(dist):
    n = jnp.maximum(dist, 0)
    exact = REL_BUCKETS // 2
    big = exact + (jnp.log(jnp.maximum(n, 1).astype(F32) / exact) / math.log(REL_MAX_DIST / exact)
                   * (REL_BUCKETS - exact)).astype(jnp.int32)
    return jnp.where(n < exact, n, jnp.minimum(big, REL_BUCKETS - 1))


def _map_query_blocks(fn, block, *xs):
    b, t = xs[0].shape[:2]
    block = min(block, t)
    nb = -(-t // block)
    pad = nb * block - t

    def split(a):
        a = jnp.pad(a, [(0, 0), (0, pad)] + [(0, 0)] * (a.ndim - 2))
        return jnp.moveaxis(a.reshape((b, nb, block) + a.shape[2:]), 1, 0)

    out = lax.map(lambda blk: fn(*blk), tuple(split(a) for a in xs))
    out = jnp.moveaxis(out, 0, 1)
    return out.reshape((b, nb * block) + out.shape[3:])[:, :t]


def _read_all(new, past, page_table):
    if past is None:
        return new
    b, n_pages = page_table.shape
    old = past[page_table].reshape((b, n_pages * PAGE_SIZE) + past.shape[2:])
    return jnp.concatenate([old.astype(new.dtype), new], axis=1)


def _read_rows(new, past, page_table, pos, *sub):
    bidx = jnp.arange(new.shape[0]).reshape((-1,) + (1,) * (pos.ndim - 1))
    p0 = 0 if past is None else page_table.shape[1] * PAGE_SIZE
    rows = new[(bidx, jnp.clip(pos - p0, 0, new.shape[1] - 1)) + sub]
    if past is None:
        return rows
    page = page_table[bidx, jnp.clip(pos // PAGE_SIZE, 0, page_table.shape[1] - 1)]
    old = past[(page, pos % PAGE_SIZE) + sub].astype(rows.dtype)
    is_old = (pos < p0).reshape(pos.shape + (1,) * (rows.ndim - pos.ndim))
    return jnp.where(is_old, old, rows)


def _dsa_attend(q, q_idx, w_idx, q_pos, k_new, v_new, kidx_new, past_k, past_v, past_kidx, page_table, bias_tbl):
    kidx_all = _read_all(kidx_new, past_kidx, page_table)
    n_keys = kidx_all.shape[1]
    n_sel = min(DSA_TOPK, n_keys // 4)
    k_pos = jnp.arange(n_keys)
    w = w_idx.astype(F32) * DSA_IDX_HEADS ** -0.5

    def block(qb, qib, wb, qp):
        dots = jnp.einsum('bthd,bsd->bths', qib, kidx_all).astype(F32) * DSA_IDX_DIM ** -0.5
        score = jnp.einsum('bths,bth->bts', jax.nn.relu(dots), wb)
        score = jnp.where(k_pos <= qp[:, :, None], score, -jnp.inf)
        _, sel = lax.top_k(score, n_sel)
        valid = sel <= qp[:, :, None]
        ks = _read_rows(k_new, past_k, page_table, sel)
        vs = _read_rows(v_new, past_v, page_table, sel)
        bias = jnp.moveaxis(bias_tbl[_rel_bucket(qp[:, :, None] - sel)], -1, 2)
        logits = jnp.einsum('bthd,btkd->bthk', qb, ks).astype(F32) * HEAD_DIM ** -0.5 + bias
        logits = jnp.where(valid[:, :, None, :], logits, -jnp.inf)
        p = jax.nn.softmax(logits, -1).astype(vs.dtype)
        return jnp.einsum('bthk,btkd->bthd', p, vs)

    return _map_query_blocks(block, DSA_QB, q, q_idx, w, q_pos)


def _moba_attend(q, q_pos, k_new, v_new, past_k, past_v, page_table, bias_tbl):
    b = q.shape[0]
    grp = jnp.arange(MOBA_HEADS) // (MOBA_HEADS // MOBA_KV_HEADS)
    k_all = _read_all(k_new, past_k, page_table)
    n_blk = k_all.shape[1] // MOBA_BLOCK
    n_top = min(MOBA_TOPK, n_blk)
    if n_top > 0:
        k_mean = k_all[:, :n_blk * MOBA_BLOCK].astype(F32).reshape(
            b, n_blk, MOBA_BLOCK, MOBA_KV_HEADS, HEAD_DIM).mean(axis=2).astype(q.dtype)
        k_mean_h = k_mean[:, :, grp]
    bias_h = bias_tbl.T
    head = jnp.arange(MOBA_HEADS)[None, None, :, None]

    def block(qblk, qp):
        nq = qblk.shape[1]
        own = qp // MOBA_BLOCK
        own_pos = own[:, :, None, None] * MOBA_BLOCK + jnp.arange(MOBA_BLOCK)
        pos = jnp.broadcast_to(own_pos, (b, nq, MOBA_HEADS, MOBA_BLOCK))
        ok = jnp.broadcast_to(own_pos <= qp[:, :, None, None], pos.shape)
        if n_top > 0:
            gs = jnp.einsum('bthd,bnhd->bthn', qblk, k_mean_h).astype(F32)
            gs = jnp.where(jnp.arange(n_blk) < own[:, :, None, None], gs, -jnp.inf)
            top_s, top_i = lax.top_k(gs, n_top)
            sel_pos = (top_i[..., None] * MOBA_BLOCK + jnp.arange(MOBA_BLOCK)).reshape(
                b, nq, MOBA_HEADS, n_top * MOBA_BLOCK)
            sel_ok = jnp.repeat(jnp.isfinite(top_s), MOBA_BLOCK, axis=-1)
            pos = jnp.concatenate([sel_pos, pos], -1)
            ok = jnp.concatenate([sel_ok, ok], -1)
        g = grp[None, None, :, None]
        ks = _read_rows(k_new, past_k, page_table, pos, g)
        vs = _read_rows(v_new, past_v, page_table, pos, g)
        bias = bias_h[head, _rel_bucket(qp[:, :, None, None] - pos)]
        logits = jnp.einsum('bthd,bthkd->bthk', qblk, ks).astype(F32) * HEAD_DIM ** -0.5 + bias
        logits = jnp.where(ok, logits, -jnp.inf)
        p = jax.nn.softmax(logits, -1).astype(vs.dtype)
        return jnp.einsum('bthk,bthkd->bthd', p, vs)

    return _map_query_blocks(block, MOBA_QB, q, q_pos)


def _mla_attend(q_lat, q_rope, q_pos, ckv_all, kr_all):
    k_pos = jnp.arange(ckv_all.shape[1])
    scale = (MLA_NOPE + MLA_ROPE) ** -0.5

    def block(ql, qr, qp):
        s = (jnp.einsum('bthc,blc->bthl', ql, ckv_all) + jnp.einsum('bthr,blr->bthl', qr, kr_all)).astype(F32) * scale
        s = jnp.where(k_pos <= qp[:, :, None, None], s, -jnp.inf)
        p = jax.nn.softmax(s, -1).astype(ckv_all.dtype)
        return jnp.einsum('bthl,blc->bthc', p, ckv_all)

    return _map_query_blocks(block, MLA_QB, q_lat, q_rope, q_pos)


def _peer(x, wq, sub_keys, u, v):
    b, t, d = x.shape

    def block(xb):
        xb = xb[0]
        n = xb.shape[0]
        q = (xb @ wq).reshape(n, PEER_HEADS, 2, PEER_KEY_DIM // 2)
        s = jnp.einsum('nhpd,phkd->nhpk', q, sub_keys).astype(F32)
        top_s, top_i = lax.top_k(s, PEER_TOPK)
        cand_s = (top_s[:, :, 0, :, None] + top_s[:, :, 1, None, :]).reshape(n, PEER_HEADS, -1)
        cand_i = (top_i[:, :, 0, :, None] * PEER_NKEYS + top_i[:, :, 1, None, :]).reshape(n, PEER_HEADS, -1)
        best_s, best = lax.top_k(cand_s, PEER_TOPK)
        expert = jnp.take_along_axis(cand_i, best, axis=-1)
        gate = jax.nn.softmax(best_s, -1)
        act = jax.nn.gelu(jnp.einsum('nd,nhkd->nhk', xb, u[expert]).astype(F32)) * gate
        return jnp.einsum('nhk,nhkd->nd', act.astype(v.dtype), v[expert])[None]

    return _map_query_blocks(block, PEER_BLOCK, x.reshape(1, b * t, d)).reshape(b, t, d)


def _layer(x, pos, past, page_table, rel_bias, params):
    (w_in, w_uq, g_cq, g_ckv, w_uk, w_uv, w_branch, w_out, ln1_g, ln1_b,
     peer_wq, peer_keys, peer_u, peer_v, ln2_g, ln2_b) = params
    (past_dk, past_dv, past_ik, past_mk, past_mv, past_ckv, past_kr) = past
    b, t, _ = x.shape
    h = x @ w_in
    (dq, dk, dv, iq, ik, iw, mq, mk, mv, cq, ckv, kr, gate) = jnp.split(
        h, np.cumsum(IN_WIDTHS)[:-1].tolist(), axis=-1)
    dq = dq.reshape(b, t, DSA_HEADS, HEAD_DIM)
    iq = iq.reshape(b, t, DSA_IDX_HEADS, DSA_IDX_DIM)
    mq = mq.reshape(b, t, MOBA_HEADS, HEAD_DIM)
    mk = mk.reshape(b, t, MOBA_KV_HEADS, HEAD_DIM)
    mv = mv.reshape(b, t, MOBA_KV_HEADS, HEAD_DIM)
    qc = (_rms_norm(cq, g_cq) @ w_uq).reshape(b, t, MLA_HEADS, MLA_NOPE + MLA_ROPE)
    q_rope = _rope(qc[..., MLA_NOPE:], pos)
    q_lat = jnp.einsum('bthn,hcn->bthc', qc[..., :MLA_NOPE], w_uk)
    ckv = _rms_norm(ckv, g_ckv)
    kr = _rope(kr, pos)
    o_a = _dsa_attend(dq, iq, iw, pos, dk, dv, ik, past_dk, past_dv, past_ik, page_table,
                      rel_bias[:, :DSA_HEADS])
    o_b = _moba_attend(mq, pos, mk, mv, past_mk, past_mv, page_table, rel_bias[:, DSA_HEADS:])
    o_lat = _mla_attend(q_lat, q_rope, pos, _read_all(ckv, past_ckv, page_table),
                        _read_all(kr, past_kr, page_table))
    o_c = jnp.einsum('bthc,hcv->bthv', o_lat, w_uv)
    branches = jnp.stack([o_a.reshape(b, t, -1), o_b.reshape(b, t, -1), o_c.reshape(b, t, -1)], axis=2)
    proj = jnp.einsum('btnw,nwd->btnd', branches, w_branch)
    gates = jax.nn.sigmoid(gate.reshape(b, t, N_BRANCH, D_MODEL))
    mixed = jnp.sum(gates * proj, axis=2) @ w_out
    x = _layer_norm(DN_ALPHA * x + mixed, ln1_g, ln1_b)
    x = _layer_norm(DN_ALPHA * x + _peer(x, peer_wq, peer_keys, peer_u, peer_v), ln2_g, ln2_b)
    return x, (dk, dv, ik, mk, mv, ckv, kr)


def setup_inputs(seed: int = 0) -> dict:
    key = jax.random.key(seed)
    ks = iter(jax.random.split(key, 64))
    n_pages = PAST_LEN // PAGE_SIZE
    n_pool = (DEC_BATCH * n_pages * 5) // 4

    def nrm(shape, scale=1.0):
        return jax.random.normal(next(ks), shape, F32) * scale

    x_prompt = nrm((BATCH, SEQ, D_MODEL))
    x_sample = nrm((DEC_BATCH, DEC_SEQ, D_MODEL))
    pool = (DEPTH, n_pool, PAGE_SIZE)
    cache_dsa_k = nrm(pool + (HEAD_DIM,))
    cache_dsa_v = nrm(pool + (HEAD_DIM,))
    cache_dsa_idx_k = nrm(pool + (DSA_IDX_DIM,))
    cache_moba_k = nrm(pool + (MOBA_KV_HEADS, HEAD_DIM))
    cache_moba_v = nrm(pool + (MOBA_KV_HEADS, HEAD_DIM))
    cache_mla_ckv = nrm(pool + (MLA_KV_LORA,))
    cache_mla_krope = nrm(pool + (MLA_ROPE,))
    page_table = jax.random.permutation(next(ks), n_pool)[:DEC_BATCH * n_pages].reshape(
        DEC_BATCH, n_pages).astype(jnp.int32)
    rel_bias = nrm((REL_BUCKETS, DSA_HEADS + MOBA_HEADS), 0.5)
    w_in = jnp.concatenate(
        [nrm((DEPTH, D_MODEL, wd), D_MODEL ** -0.5 * (DN_BETA if i in IN_VALUE_GROUPS else 1.0))
         for i, wd in enumerate(IN_WIDTHS)], axis=-1)
    w_uq = nrm((DEPTH, MLA_Q_LORA, MLA_HEADS * (MLA_NOPE + MLA_ROPE)), MLA_Q_LORA ** -0.5)
    g_cq = 1.0 + nrm((DEPTH, MLA_Q_LORA), 0.02)
    g_ckv = 1.0 + nrm((DEPTH, MLA_KV_LORA), 0.02)
    w_uk = nrm((DEPTH, MLA_HEADS, MLA_KV_LORA, MLA_NOPE), MLA_KV_LORA ** -0.5)
    w_uv = nrm((DEPTH, MLA_HEADS, MLA_KV_LORA, MLA_V), MLA_KV_LORA ** -0.5 * DN_BETA)
    w_branch = nrm((DEPTH, N_BRANCH, BRANCH_WIDTH, D_MODEL), BRANCH_WIDTH ** -0.5 * DN_BETA)
    w_out = nrm((DEPTH, D_MODEL, D_MODEL), D_MODEL ** -0.5 * DN_BETA)
    ln1_g = 1.0 + nrm((DEPTH, D_MODEL), 0.02)
    ln1_b = nrm((DEPTH, D_MODEL), 0.02)
    peer_wq = nrm((DEPTH, D_MODEL, PEER_HEADS * PEER_KEY_DIM), D_MODEL ** -0.5)
    peer_keys = nrm((DEPTH, 2, PEER_HEADS, PEER_NKEYS, PEER_KEY_DIM // 2), (PEER_KEY_DIM // 2) ** -0.5)
    peer_u = nrm((DEPTH, PEER_EXPERTS, D_MODEL), D_MODEL ** -0.5 * DN_BETA)
    peer_v = nrm((DEPTH, PEER_EXPERTS, D_MODEL), DN_BETA)
    ln2_g = 1.0 + nrm((DEPTH, D_MODEL), 0.02)
    ln2_b = nrm((DEPTH, D_MODEL), 0.02)
    return {'x_prompt': x_prompt, 'x_sample': x_sample,
            'cache_dsa_k': cache_dsa_k, 'cache_dsa_v': cache_dsa_v, 'cache_dsa_idx_k': cache_dsa_idx_k,
            'cache_moba_k': cache_moba_k, 'cache_moba_v': cache_moba_v,
            'cache_mla_ckv': cache_mla_ckv, 'cache_mla_krope': cache_mla_krope,
            'page_table': page_table, 'rel_bias': rel_bias,
            'w_in': w_in, 'w_uq': w_uq, 'g_cq': g_cq, 'g_ckv': g_ckv, 'w_uk': w_uk, 'w_uv': w_uv,
            'w_branch': w_branch, 'w_out': w_out, 'ln1_g': ln1_g, 'ln1_b': ln1_b,
            'peer_wq': peer_wq, 'peer_keys': peer_keys, 'peer_u': peer_u, 'peer_v': peer_v,
            'ln2_g': ln2_g, 'ln2_b': ln2_b}


def reference(x_prompt, x_sample, cache_dsa_k, cache_dsa_v, cache_dsa_idx_k, cache_moba_k, cache_moba_v,
              cache_mla_ckv, cache_mla_krope, page_table, rel_bias, w_in, w_uq, g_cq, g_ckv, w_uk, w_uv,
              w_branch, w_out, ln1_g, ln1_b, peer_wq, peer_keys, peer_u, peer_v, ln2_g, ln2_b):
    past_len = page_table.shape[1] * PAGE_SIZE
    pos_p = jnp.broadcast_to(jnp.arange(x_prompt.shape[1], dtype=jnp.int32), x_prompt.shape[:2])
    pos_s = jnp.broadcast_to(past_len + jnp.arange(x_sample.shape[1], dtype=jnp.int32), x_sample.shape[:2])
    h_p, h_s = x_prompt, x_sample
    rows_p, rows_s = [], []
    no_past = (None, None, None, None, None, None, None)
    for l in range(DEPTH):
        params = (w_in[l], w_uq[l], g_cq[l], g_ckv[l], w_uk[l], w_uv[l], w_branch[l], w_out[l],
                  ln1_g[l], ln1_b[l], peer_wq[l], peer_keys[l], peer_u[l], peer_v[l], ln2_g[l], ln2_b[l])
        h_p, r_p = _layer(h_p, pos_p, no_past, None, rel_bias, params)
        past = (cache_dsa_k[l], cache_dsa_v[l], cache_dsa_idx_k[l], cache_moba_k[l], cache_moba_v[l],
                cache_mla_ckv[l], cache_mla_krope[l])
        h_s, r_s = _layer(h_s, pos_s, past, page_table, rel_bias, params)
        rows_p.append(r_p)
        rows_s.append(r_s)

    def stack(rows, i):
        return jnp.stack([r[i] for r in rows])

    new_dsa_k_prompt, new_dsa_k_sample = stack(rows_p, 0), stack(rows_s, 0)
    new_dsa_v_prompt, new_dsa_v_sample = stack(rows_p, 1), stack(rows_s, 1)
    new_dsa_idx_k_prompt, new_dsa_idx_k_sample = stack(rows_p, 2), stack(rows_s, 2)
    new_moba_k_prompt, new_moba_k_sample = stack(rows_p, 3), stack(rows_s, 3)
    new_moba_v_prompt, new_moba_v_sample = stack(rows_p, 4), stack(rows_s, 4)
    new_mla_ckv_prompt, new_mla_ckv_sample = stack(rows_p, 5), stack(rows_s, 5)
    new_mla_krope_prompt, new_mla_krope_sample = stack(rows_p, 6), stack(rows_s, 6)
    return (h_p, h_s, new_dsa_k_prompt, new_dsa_k_sample, new_dsa_v_prompt, new_dsa_v_sample,
            new_dsa_idx_k_prompt, new_dsa_idx_k_sample, new_moba_k_prompt, new_moba_k_sample,
            new_moba_v_prompt, new_moba_v_sample, new_mla_ckv_prompt, new_mla_ckv_sample,
            new_mla_krope_prompt, new_mla_krope_sample)
```

```python
import functools
import math

import jax
import jax.numpy as jnp
import numpy as np
from jax import lax
from jax.experimental import pallas as pl
from jax.experimental.pallas import tpu as pltpu

F32 = jnp.float32
BF16 = jnp.bfloat16
I32 = jnp.int32

D_MODEL = 1024
PAGE_SIZE = 128
HEAD_DIM = 64
N_HEADS = 8
DSA_IDX_DIM = 64
DSA_TOPK = 256
MOBA_KV_HEADS = 2
MOBA_BLOCK = 256
MOBA_TOPK = 3
MLA_Q_LORA = 256
MLA_KV_LORA = 128
MLA_NOPE = 64
MLA_ROPE = 32
MLA_QK = MLA_KV_LORA + MLA_ROPE
MLA_QK_PAD = 256
ROPE_THETA = 10000.0
N_BRANCH = 3
BRANCH_WIDTH = 512
REL_BUCKETS = 32
REL_MAX_DIST = 128
PEER_HEADS = 8
PEER_NKEYS = 128
PEER_KEY_DIM = 256
PEER_TOPK = 16
LN_EPS = 1e-5
RMS_EPS = 1e-6
NEG = -1e30

VMEM_LIMIT_BYTES = 56 * 1024 * 1024
ATT_TILE = 256

IN_WIDTHS = (512, 64, 64, 512, 64, 8, 512, 128, 128, 256, 128, 32, 3072)
IN_OFFS = tuple(int(v) for v in np.cumsum((0,) + IN_WIDTHS))

T_DQ, T_IQ, T_MQ, T_CQ, T_CKV, T_MV, T_DV, T_IW, T_ROWS = 0, 512, 1024, 1536, 1792, 1920, 2048, 2112, 2128
N_MK, N_MV, N_CKV, N_DK, N_DV, N_IK, N_KR, N_KRS, N_COLS = 0, 128, 256, 384, 512, 640, 768, 896, 1024


def _split_bf16(a):
    hi = a.astype(BF16)
    lo = (a - hi.astype(F32)).astype(BF16)
    return hi, lo


def _dot(a, b):
    return jnp.dot(a, b, preferred_element_type=F32)


def _dot_nt(a, b):
    return lax.dot_general(a, b, (((1,), (1,)), ((), ())), preferred_element_type=F32)


def _dot3(a_hi, a_lo, b_hi, b_lo, dot):
    return dot(a_hi, b_hi) + dot(a_hi, b_lo) + dot(a_lo, b_hi)


def _cparams(sem, vmem=VMEM_LIMIT_BYTES):
    return pltpu.CompilerParams(dimension_semantics=sem, vmem_limit_bytes=vmem)


def _proj_kernel(x_ref, posr_ref, posc_ref, wn_hi_ref, wn_lo_ref, wt_hi_ref, wt_lo_ref,
                 gcq_ref, gckv_c_ref, gckv_r_ref, wuq_ref, wuk_ref, fcol_ref, frow_ref, srow_ref,
                 mk_ref, mv_ref, ckv_ref, dk_ref, dv_ref, ik_ref, kr_ref,
                 mkb_ref, dkb_ref, ik3_ref, kc_ref, km_ref,
                 dqt_ref, iq3t_ref, mqt_ref, mq3t_ref, qt_ref, ckvt_ref, mvt_ref, dvt_ref, wt_ref):
    x = x_ref[...]
    x_hi, x_lo = _split_bf16(x)
    h = _dot3(x_hi, x_lo, wn_hi_ref[...], wn_lo_ref[...], _dot)
    ht = _dot3(wt_hi_ref[...], wt_lo_ref[...], x_hi, x_lo, _dot_nt)

    mk = h[:, N_MK:N_MK + 128]
    mk_ref[...] = mk
    mkb_ref[0] = mk[:, :64].astype(BF16)
    mkb_ref[1] = mk[:, 64:].astype(BF16)
    km_ref[...] = jnp.mean(mk, axis=0, keepdims=True)
    mv_ref[...] = h[:, N_MV:N_MV + 128]
    c = h[:, N_CKV:N_CKV + 128]
    ckv = c * lax.rsqrt(jnp.mean(c * c, axis=-1, keepdims=True) + RMS_EPS) * gckv_r_ref[...]
    ckv_ref[...] = ckv
    dk = h[:, N_DK:N_DK + 64]
    dk_ref[...] = dk
    dkb_ref[...] = dk.astype(BF16)
    dv_ref[...] = h[:, N_DV:N_DV + 64]
    ik2 = h[:, N_IK:N_IK + 128]
    ik_ref[...] = ik2[:, :64]
    ik_hi = ik2.astype(BF16)
    ik_lo = (ik2 - ik_hi.astype(F32)).astype(BF16)
    left = lax.broadcasted_iota(I32, ik2.shape, 1) < 64
    zero = jnp.zeros_like(ik_hi)
    ik3_ref[...] = jnp.concatenate([jnp.where(left, ik_hi, ik_lo), jnp.where(left, ik_hi, zero)], axis=1)
    ang = posc_ref[...] * frow_ref[...]
    krr = h[:, N_KR:N_KR + 128] * jnp.cos(ang) + h[:, N_KRS:N_KRS + 128] * (jnp.sin(ang) * srow_ref[...])
    kr_ref[...] = krr[:, :MLA_ROPE]
    kc_ref[...] = jnp.concatenate([ckv, krr], axis=1).astype(BF16)

    dqt_ref[...] = (ht[T_DQ:T_DQ + 512] * (HEAD_DIM ** -0.5)).astype(BF16)
    mq = ht[T_MQ:T_MQ + 512]
    mqt_ref[...] = (mq * (HEAD_DIM ** -0.5)).astype(BF16)
    iq = ht[T_IQ:T_IQ + 512]
    for src, dst in ((iq, iq3t_ref), (mq, mq3t_ref)):
        for hd in range(N_HEADS):
            q = src[hd * 64:(hd + 1) * 64]
            q_hi, q_lo = _split_bf16(q)
            dst[hd] = jnp.concatenate([q_hi, q_hi, q_lo, jnp.zeros_like(q_hi)], axis=0)
    wt_ref[...] = ht[T_IW:T_IW + 8] * (N_HEADS ** -0.5) * (DSA_IDX_DIM ** -0.5)
    mvt_ref[...] = ht[T_MV:T_MV + 128].astype(BF16)
    dvt_ref[...] = ht[T_DV:T_DV + 64].astype(BF16)
    ct = ht[T_CKV:T_CKV + 128]
    ckvt = ct * lax.rsqrt(jnp.mean(ct * ct, axis=0, keepdims=True) + RMS_EPS) * gckv_c_ref[...]
    ckvt_ref[...] = ckvt.astype(BF16)
    cq = ht[T_CQ:T_CQ + 256]
    cqn = cq * lax.rsqrt(jnp.mean(cq * cq, axis=0, keepdims=True) + RMS_EPS) * gcq_ref[...]
    qc = _dot(wuq_ref[...], cqn.astype(BF16))
    angt = fcol_ref[...] * posr_ref[...]
    cs, sn = jnp.cos(angt), jnp.sin(angt)
    scale = (MLA_NOPE + MLA_ROPE) ** -0.5
    half = MLA_ROPE // 2
    for hd in range(N_HEADS):
        base = hd * (MLA_NOPE + MLA_ROPE)
        nope = qc[base:base + MLA_NOPE]
        r1 = qc[base + MLA_NOPE:base + MLA_NOPE + half]
        r2 = qc[base + MLA_NOPE + half:base + MLA_NOPE + MLA_ROPE]
        ql = _dot(wuk_ref[hd], nope.astype(BF16))
        qrows = jnp.concatenate([ql, r1 * cs - r2 * sn, r2 * cs + r1 * sn], axis=0) * scale
        qt_ref[hd] = jnp.concatenate([qrows.astype(BF16), jnp.zeros((MLA_QK_PAD - MLA_QK, qrows.shape[1]), BF16)], axis=0)


def _rope_tables():
    half = MLA_ROPE // 2
    freq = ROPE_THETA ** (-jnp.arange(half, dtype=F32) / half)
    frow = jnp.zeros((1, 128), F32).at[0, :MLA_ROPE].set(jnp.concatenate([freq, freq]))
    srow = jnp.zeros((1, 128), F32).at[0, :MLA_ROPE].set(
        jnp.concatenate([-jnp.ones((half,), F32), jnp.ones((half,), F32)]))
    return freq.reshape(half, 1), frow, srow


def _prep_proj_weights(w_in, w_uq, w_uk):
    def col(i):
        return w_in[:, IN_OFFS[i]:IN_OFFS[i + 1]]

    dq, dk, dv, iq, ik, iw, mq, mk, mv, cq, ckv, kr = (col(i) for i in range(12))
    z = lambda n: jnp.zeros((D_MODEL, n), F32)
    half = MLA_ROPE // 2
    kr_sw = jnp.concatenate([kr[:, half:], kr[:, :half]], axis=1)
    wn = jnp.concatenate([mk, mv, ckv, dk, z(64), dv, z(64), ik, ik, kr, z(96), kr_sw, z(96)], axis=1)
    wt = jnp.concatenate([dq, iq, mq, cq, ckv, mv, dv, iw, z(T_ROWS - T_IW - 8)], axis=1).T
    wn_hi, wn_lo = _split_bf16(wn)
    wt_hi, wt_lo = _split_bf16(wt)
    return wn_hi, wn_lo, wt_hi, wt_lo, w_uq.T.astype(BF16), w_uk.astype(BF16)


def _project(x, pos, pw, g_cq, g_ckv, tm):
    b, t, _ = x.shape
    wn_hi, wn_lo, wt_hi, wt_lo, wuq_t, wuk = pw
    fcol, frow, srow = _rope_tables()
    nt = t // tm
    grid = (b, nt)
    tok = lambda w: pl.BlockSpec((None, tm, w), lambda bi, i: (bi, i, 0))
    feat = lambda r: pl.BlockSpec((None, r, tm), lambda bi, i: (bi, 0, i))
    feat_c = lambda r: pl.BlockSpec((None, None, r, tm), lambda bi, i: (bi, i, 0, 0))
    feat_h = lambda r: pl.BlockSpec((None, N_HEADS, r, tm), lambda bi, i: (bi, 0, 0, i))
    full = lambda a: pl.BlockSpec(a.shape, lambda bi, i: (0,) * a.ndim)
    sds = jax.ShapeDtypeStruct
    consts = (wn_hi, wn_lo, wt_hi, wt_lo, g_cq.reshape(-1, 1), g_ckv.reshape(-1, 1), g_ckv.reshape(1, -1),
              wuq_t, wuk, fcol, frow, srow)
    out_specs = [tok(128), tok(128), tok(128), tok(64), tok(64), tok(64), tok(MLA_ROPE),
                 pl.BlockSpec((None, 2, tm, 64), lambda bi, i: (bi, 0, i, 0)), tok(64), tok(256), tok(256),
                 pl.BlockSpec((None, None, 1, 128), lambda bi, i: (bi, i, 0, 0)),
                 feat(512), feat_h(256), feat(512), feat_h(256), feat_h(MLA_QK_PAD), feat_c(128), feat_c(128),
                 feat_c(64), feat(8)]
    out_shape = [sds((b, t, 128), F32), sds((b, t, 128), F32), sds((b, t, 128), F32), sds((b, t, 64), F32),
                 sds((b, t, 64), F32), sds((b, t, 64), F32), sds((b, t, MLA_ROPE), F32),
                 sds((b, 2, t, 64), BF16), sds((b, t, 64), BF16), sds((b, t, 256), BF16), sds((b, t, 256), BF16),
                 sds((b, nt, 1, 128), F32),
                 sds((b, 512, t), BF16), sds((b, N_HEADS, 256, t), BF16), sds((b, 512, t), BF16),
                 sds((b, N_HEADS, 256, t), BF16), sds((b, N_HEADS, MLA_QK_PAD, t), BF16), sds((b, nt, 128, tm), BF16),
                 sds((b, nt, 128, tm), BF16), sds((b, nt, 64, tm), BF16), sds((b, 8, t), F32)]
    outs = pl.pallas_call(
        _proj_kernel,
        grid=grid,
        in_specs=[tok(D_MODEL), pl.BlockSpec((None, 1, tm), lambda bi, i: (bi, 0, i)), tok(1)]
        + [full(a) for a in consts],
        out_specs=out_specs,
        out_shape=out_shape,
        compiler_params=_cparams(("parallel", "parallel")),
        name="proj",
    )(x, pos.reshape(b, 1, t), pos.reshape(b, t, 1), *consts)
    names = ("mk", "mv", "ckv", "dk", "dv", "ik", "kr", "mkb", "dkb", "ik3", "kc", "km",
             "dqt", "iq3t", "mqt", "mq3t", "qt", "ckvt", "mvt", "dvt", "wt")
    return dict(zip(names, outs))


def _rel_bucket(dist):
    n = jnp.maximum(dist, 0)
    exact = REL_BUCKETS // 2
    big = exact + (jnp.log(jnp.maximum(n, 1).astype(F32) / exact) / math.log(REL_MAX_DIST / exact)
                   * (REL_BUCKETS - exact)).astype(I32)
    return jnp.where(n < exact, n, jnp.minimum(big, REL_BUCKETS - 1))


def _bias_tiles(bias_tbl, tile):
    kk = jnp.arange(tile)[:, None]
    qq = jnp.arange(tile)[None, :]
    d0 = bias_tbl[_rel_bucket(qq - kk)]
    d1 = bias_tbl[_rel_bucket(tile + qq - kk)]
    far = bias_tbl[_rel_bucket(jnp.full((1, tile), 2 * tile))]
    mv = lambda a: jnp.moveaxis(a, -1, 0)
    return mv(d0), mv(d1), mv(far)


def _flash_init(m_ref, l_ref, acc_ref):
    m_ref[...] = jnp.full(m_ref.shape, NEG, F32)
    l_ref[...] = jnp.zeros(l_ref.shape, F32)
    acc_ref[...] = jnp.zeros(acc_ref.shape, F32)


def _flash_step(s, mask, hd, vt, m_ref, l_ref, acc_ref):
    if mask is not None:
        s = jnp.where(mask, s, NEG)
    m_old = m_ref[hd:hd + 1, :]
    m_new = jnp.maximum(m_old, jnp.max(s, axis=0, keepdims=True))
    p = jnp.exp(s - m_new)
    if mask is not None:
        p = jnp.where(mask, p, 0.0)
    alpha = jnp.exp(m_old - m_new)
    l_ref[hd:hd + 1, :] = l_ref[hd:hd + 1, :] * alpha + jnp.sum(p, axis=0, keepdims=True)
    acc_ref[hd] = acc_ref[hd] * alpha + _dot(vt, p.astype(BF16))
    m_ref[hd:hd + 1, :] = m_new


def _flash_finish(o_ref, l_ref, acc_ref, width):
    for hd in range(N_HEADS):
        o = acc_ref[hd] / l_ref[hd:hd + 1, :]
        o_ref[:, hd * width:(hd + 1) * width] = o.T


def _causal_mask(tile):
    return lax.broadcasted_iota(I32, (tile, tile), 0) <= lax.broadcasted_iota(I32, (tile, tile), 1)


def _mla_prompt_kernel(qt_ref, kc_ref, vt_ref, o_ref, m_ref, l_ref, acc_ref):
    i = pl.program_id(1)
    tile = qt_ref.shape[-1]
    _flash_init(m_ref, l_ref, acc_ref)

    def chunk(c, mask):
        k = kc_ref[pl.ds(pl.multiple_of(c * tile, tile), tile), :]
        vt = vt_ref[c]
        for hd in range(N_HEADS):
            _flash_step(_dot(k, qt_ref[hd]), mask, hd, vt, m_ref, l_ref, acc_ref)

    def body(c, carry):
        chunk(c, None)
        return carry

    lax.fori_loop(0, i, body, 0)
    chunk(i, _causal_mask(tile))
    _flash_finish(o_ref, l_ref, acc_ref, MLA_KV_LORA)


def _mla_prompt(p):
    b, t, _ = p["kc"].shape
    tile = ATT_TILE
    nc = t // tile
    return pl.pallas_call(
        _mla_prompt_kernel,
        grid=(b, nc),
        in_specs=[pl.BlockSpec((None, N_HEADS, MLA_QK_PAD, tile), lambda bi, i: (bi, 0, 0, i)),
                  pl.BlockSpec((None, t, 256), lambda bi, i: (bi, 0, 0)),
                  pl.BlockSpec((None, nc, MLA_KV_LORA, tile), lambda bi, i: (bi, 0, 0, 0))],
        out_specs=pl.BlockSpec((None, tile, N_HEADS * MLA_KV_LORA), lambda bi, i: (bi, i, 0)),
        out_shape=jax.ShapeDtypeStruct((b, t, N_HEADS * MLA_KV_LORA), F32),
        scratch_shapes=[pltpu.VMEM((N_HEADS, tile), F32), pltpu.VMEM((N_HEADS, tile), F32),
                        pltpu.VMEM((N_HEADS, MLA_KV_LORA, tile), F32)],
        compiler_params=_cparams(("parallel", "parallel")),
        name="mla_prompt",
    )(p["qt"], p["kc"], p["ckvt"])


INT_MIN = -2 ** 31
INT_MAX = 2 ** 31 - 1


def _order_key(s):
    bits = pltpu.bitcast(jnp.where(s == 0.0, 0.0, s), I32)
    return jnp.where(bits < 0, bits ^ INT_MAX, bits)


def _topk_threshold(count_ge, count_gt_eq, count_eq_below, k, lanes, idx_bits):
    def bit_step(n, t):
        cand = t ^ jnp.left_shift(jnp.int32(1), 31 - n)
        return jnp.where(count_ge(cand) >= k, cand, t)

    t = lax.fori_loop(0, 32, bit_step, jnp.full((1, lanes), INT_MIN, I32))
    n_gt, n_eq = count_gt_eq(t)
    need = k - n_gt
    excess = (n_eq > need) & (t != INT_MIN)

    def resolve():
        def idx_step(n, j):
            cand = j | jnp.left_shift(jnp.int32(1), idx_bits - 1 - n)
            return jnp.where(count_eq_below(t, cand) < need, cand, j)

        j = lax.fori_loop(0, idx_bits, idx_step, jnp.zeros((1, lanes), I32))
        return jnp.where(excess, j, INT_MAX)

    j = lax.cond(jnp.any(excess), resolve, lambda: jnp.full((1, lanes), INT_MAX, I32))
    return t, j


def _dsa_prompt_kernel(iq3_ref, w_ref, ik3_ref, qt_ref, k_ref, vt_ref, d0_ref, d1_ref, far_ref, o_ref,
                       key_ref, m_ref, l_ref, acc_ref):
    i = pl.program_id(1)
    tile = qt_ref.shape[-1]
    _flash_init(m_ref, l_ref, acc_ref)
    causal = _causal_mask(tile)
    rows = lambda c: pl.ds(pl.multiple_of(c * tile, tile), tile)

    def score(c, carry):
        ik3 = ik3_ref[rows(c), :]
        s = jnp.zeros((tile, tile), F32)
        for hd in range(N_HEADS):
            s = s + jnp.maximum(_dot(ik3, iq3_ref[hd]), 0.0) * w_ref[hd:hd + 1, :]
        key = _order_key(s)
        key_ref[c] = jnp.where((c < i) | causal, key, INT_MIN)
        return carry

    lax.fori_loop(0, i + 1, score, 0)

    def count(pred):
        def body(c, acc):
            return acc + jnp.sum(pred(key_ref[c], c).astype(I32), axis=0, keepdims=True)
        return lax.fori_loop(0, i + 1, body, jnp.zeros((1, tile), I32))

    def count_gt_eq(t):
        def body(c, acc):
            key = key_ref[c]
            return (acc[0] + jnp.sum((key > t).astype(I32), axis=0, keepdims=True),
                    acc[1] + jnp.sum((key == t).astype(I32), axis=0, keepdims=True))
        z = jnp.zeros((1, tile), I32)
        return lax.fori_loop(0, i + 1, body, (z, z))

    kidx = lambda c: c * tile + lax.broadcasted_iota(I32, (tile, tile), 0)
    nbits = max(1, (key_ref.shape[0] * tile - 1).bit_length())
    t, j = _topk_threshold(
        lambda cand: count(lambda key, c: key >= cand),
        count_gt_eq,
        lambda tt, cand: count(lambda key, c: (key == tt) & (kidx(c) < cand)),
        DSA_TOPK, tile, nbits)

    def chunk(c, kind):
        key = key_ref[c]
        mask = (key > t) | ((key == t) & (kidx(c) <= j))
        if kind == "diag":
            mask = mask & causal
        k = k_ref[rows(c), :]
        vt = vt_ref[c]
        for hd in range(N_HEADS):
            s = _dot(k, qt_ref[hd * 64:(hd + 1) * 64, :])
            s = s + {"diag": d0_ref, "prev": d1_ref, "far": far_ref}[kind][hd]
            _flash_step(s, mask, hd, vt, m_ref, l_ref, acc_ref)

    def body(c, carry):
        chunk(c, "far")
        return carry

    lax.fori_loop(0, jnp.maximum(i - 1, 0), body, 0)

    @pl.when(i >= 1)
    def _():
        chunk(i - 1, "prev")

    chunk(i, "diag")
    _flash_finish(o_ref, l_ref, acc_ref, HEAD_DIM)


def _dsa_prompt(p, bias_tbl):
    b, t, _ = p["dk"].shape
    tile = ATT_TILE
    nc = t // tile
    d0, d1, far = _bias_tiles(bias_tbl, tile)
    const = lambda a: pl.BlockSpec(a.shape, lambda bi, i: (0,) * a.ndim)
    return pl.pallas_call(
        _dsa_prompt_kernel,
        grid=(b, nc),
        in_specs=[pl.BlockSpec((None, N_HEADS, 256, tile), lambda bi, i: (bi, 0, 0, i)),
                  pl.BlockSpec((None, 8, tile), lambda bi, i: (bi, 0, i)),
                  pl.BlockSpec((None, t, 256), lambda bi, i: (bi, 0, 0)),
                  pl.BlockSpec((None, 512, tile), lambda bi, i: (bi, 0, i)),
                  pl.BlockSpec((None, t, 64), lambda bi, i: (bi, 0, 0)),
                  pl.BlockSpec((None, nc, 64, tile), lambda bi, i: (bi, 0, 0, 0)),
                  const(d0), const(d1), const(far)],
        out_specs=pl.BlockSpec((None, tile, N_HEADS * HEAD_DIM), lambda bi, i: (bi, i, 0)),
        out_shape=jax.ShapeDtypeStruct((b, t, N_HEADS * HEAD_DIM), F32),
        scratch_shapes=[pltpu.VMEM((nc, tile, tile), I32),
                        pltpu.VMEM((N_HEADS, tile), F32), pltpu.VMEM((N_HEADS, tile), F32),
                        pltpu.VMEM((N_HEADS, HEAD_DIM, tile), F32)],
        compiler_params=_cparams(("parallel", "parallel")),
        name="dsa_prompt",
    )(p["iq3t"], p["wt"], p["ik3"], p["dqt"], p["dkb"], p["dvt"], d0, d1, far)


def _moba_prompt_kernel(q3_ref, qt_ref, k_ref, vt_ref, km_ref, d0_ref, d1_ref, far_ref, o_ref,
                        sel_ref, m_ref, l_ref, acc_ref):
    i = pl.program_id(1)
    tile = qt_ref.shape[-1]
    nblk = km_ref.shape[0]
    _flash_init(m_ref, l_ref, acc_ref)

    blk = lax.broadcasted_iota(I32, (nblk, tile), 0)
    valid = blk < i
    km = km_ref[...]
    for g in range(MOBA_KV_HEADS):
        km_hi, km_lo = _split_bf16(km[:, g * 64:(g + 1) * 64])
        for hd in range(g * 4, g * 4 + 4):
            q_hi = q3_ref[hd, 0:64, :]
            q_lo = q3_ref[hd, 128:192, :]
            gs = _dot(km_hi, q_hi) + _dot(km_lo, q_hi) + _dot(km_hi, q_lo)
            cur = jnp.where(valid, gs, -jnp.inf)
            sel = jnp.zeros((nblk, tile), F32)
            for _ in range(MOBA_TOPK):
                mx = jnp.max(cur, axis=0, keepdims=True)
                first = jnp.min(jnp.where(cur == mx, blk, nblk), axis=0, keepdims=True)
                pick = (blk == first) & (mx > -jnp.inf)
                sel = jnp.where(pick, 1.0, sel)
                cur = jnp.where(pick, -jnp.inf, cur)
            sel_ref[hd] = sel

    def chunk(c, kind):
        off = pl.ds(pl.multiple_of(c * tile, tile), tile)
        vt_all = vt_ref[c]
        for g in range(MOBA_KV_HEADS):
            k = k_ref[g, off, :]
            vt = vt_all[g * 64:(g + 1) * 64]
            for hd in range(g * 4, g * 4 + 4):
                s = _dot(k, qt_ref[hd * 64:(hd + 1) * 64, :])
                if kind == "diag":
                    s = s + d0_ref[hd]
                    mask = _causal_mask(tile)
                else:
                    s = s + (d1_ref[hd] if kind == "prev" else far_ref[hd])
                    mask = jnp.broadcast_to(sel_ref[hd, pl.ds(c, 1), :] > 0.5, (tile, tile))
                _flash_step(s, mask, hd, vt, m_ref, l_ref, acc_ref)

    def body(c, carry):
        chunk(c, "far")
        return carry

    lax.fori_loop(0, jnp.maximum(i - 1, 0), body, 0)

    @pl.when(i >= 1)
    def _():
        chunk(i - 1, "prev")

    chunk(i, "diag")
    _flash_finish(o_ref, l_ref, acc_ref, HEAD_DIM)


def _moba_prompt(p, bias_tbl):
    b, t, _ = p["mk"].shape
    tile = ATT_TILE
    nc = t // tile
    d0, d1, far = _bias_tiles(bias_tbl, tile)
    km = p["km"].reshape(b, nc, 128)
    const = lambda a: pl.BlockSpec(a.shape, lambda bi, i: (0,) * a.ndim)
    return pl.pallas_call(
        _moba_prompt_kernel,
        grid=(b, nc),
        in_specs=[pl.BlockSpec((None, N_HEADS, 256, tile), lambda bi, i: (bi, 0, 0, i)),
                  pl.BlockSpec((None, 512, tile), lambda bi, i: (bi, 0, i)),
                  pl.BlockSpec((None, 2, t, 64), lambda bi, i: (bi, 0, 0, 0)),
                  pl.BlockSpec((None, nc, 128, tile), lambda bi, i: (bi, 0, 0, 0)),
                  pl.BlockSpec((None, nc, 128), lambda bi, i: (bi, 0, 0)),
                  const(d0), const(d1), const(far)],
        out_specs=pl.BlockSpec((None, tile, N_HEADS * HEAD_DIM), lambda bi, i: (bi, i, 0)),
        out_shape=jax.ShapeDtypeStruct((b, t, N_HEADS * HEAD_DIM), F32),
        scratch_shapes=[pltpu.VMEM((N_HEADS, nc, tile), F32),
                        pltpu.VMEM((N_HEADS, tile), F32), pltpu.VMEM((N_HEADS, tile), F32),
                        pltpu.VMEM((N_HEADS, HEAD_DIM, tile), F32)],
        compiler_params=_cparams(("parallel", "parallel")),
        name="moba_prompt",
    )(p["mq3t"], p["mqt"], p["mkb"], p["mvt"], km, d0, d1, far)


PAGES_PER_STEP = 16


def _page_specs(width, pages, n_chunks, layer, chunk_of):
    def spec(s):
        def index_map(b, j, pt):
            c = jnp.clip(chunk_of(j), 0, n_chunks - 1)
            return (layer, pt[b * (n_chunks * pages) + c * pages + s], 0, 0)
        return pl.BlockSpec((None, None, PAGE_SIZE, width), index_map)
    return [spec(s) for s in range(pages)]


def _cat_pages(refs):
    return jnp.concatenate([r[...] for r in refs], axis=0)


def _row_flash_step(s, mask, v, m_ref, l_ref, acc_ref, rows=slice(None)):
    if mask is not None:
        s = jnp.where(mask, s, NEG)
    m_old = m_ref[rows, :]
    m_new = jnp.maximum(m_old, jnp.max(s, axis=1, keepdims=True))
    p = jnp.exp(s - m_new)
    if mask is not None:
        p = jnp.where(mask, p, 0.0)
    alpha = jnp.exp(m_old - m_new)
    l_ref[rows, :] = l_ref[rows, :] * alpha + jnp.sum(p, axis=1, keepdims=True)
    acc_ref[rows, :] = acc_ref[rows, :] * alpha + _dot(p.astype(BF16), v)
    m_ref[rows, :] = m_new


def _row_flash_last(s_new, include, v_new, m_ref, l_ref, acc_ref, o_ref):
    m_old = m_ref[...]
    m_new = jnp.where(include, jnp.maximum(m_old, s_new), m_old)
    p = jnp.where(include, jnp.exp(s_new - m_new), 0.0)
    alpha = jnp.exp(m_old - m_new)
    l = l_ref[...] * alpha + p
    o_ref[...] = (acc_ref[...] * alpha + p * v_new) / l


def _dsa_sample_kernel(k_sel, n_chunks, pages, *refs):
    pt_ref = refs[0]
    ik_pages = refs[1:1 + pages]
    k_pages = refs[1 + pages:1 + 2 * pages]
    v_pages = refs[1 + 2 * pages:1 + 3 * pages]
    (iq3_ref, w_ref, ikn_ref, dq_ref, dkn_ref, dvn_ref, bias_ref, bias0_ref, o_ref,
     key_ref, newkey_ref, t_ref, j_ref, m_ref, l_ref, acc_ref) = refs[1 + 3 * pages:]
    del pt_ref
    j = pl.program_id(1)
    span = pages * PAGE_SIZE
    q_hi = iq3_ref[:, 0:64]
    q_lo = iq3_ref[:, 128:192]
    lane = lax.broadcasted_iota(I32, (1, span), 1)

    @pl.when(j < n_chunks)
    def _():
        k_hi, k_lo = _split_bf16(_cat_pages(ik_pages))
        dots = _dot3(q_hi, q_lo, k_hi, k_lo, _dot_nt)
        score = jnp.sum(jnp.maximum(dots, 0.0) * w_ref[...], axis=0, keepdims=True)
        key_ref[j] = _order_key(score)

    @pl.when(j == n_chunks - 1)
    def _():
        iq = q_hi.astype(F32) + q_lo.astype(F32)
        dots = jnp.sum(iq * ikn_ref[...], axis=1, keepdims=True)
        newkey = _order_key(jnp.sum(jnp.maximum(dots, 0.0) * w_ref[...], axis=0, keepdims=True))
        newkey_ref[...] = newkey
        new_idx = n_chunks * span

        def count(pred):
            tot = pred(newkey, new_idx).astype(I32)
            for c in range(n_chunks):
                tot = tot + jnp.sum(pred(key_ref[c], c * span + lane).astype(I32), axis=1, keepdims=True)
            return tot

        t, jj = _topk_threshold(
            lambda cand: count(lambda key, idx: key >= cand),
            lambda tt: (count(lambda key, idx: key > tt), count(lambda key, idx: key == tt)),
            lambda tt, cand: count(lambda key, idx: (key == tt) & (idx < cand)),
            k_sel, 1, max(1, new_idx.bit_length()))
        t_ref[...] = t
        j_ref[...] = jj
        _flash_init(m_ref, l_ref, acc_ref)

    @pl.when(j >= n_chunks)
    def _():
        c = j - n_chunks
        t, jj = t_ref[...], j_ref[...]
        key = key_ref[c]
        mask = (key > t) | ((key == t) & (c * span + lane <= jj))
        s = _dot_nt(dq_ref[...], _cat_pages(k_pages).astype(BF16)) + bias_ref[...]
        _row_flash_step(s, mask, _cat_pages(v_pages).astype(BF16), m_ref, l_ref, acc_ref)

    @pl.when(j == 2 * n_chunks - 1)
    def _():
        t, jj = t_ref[...], j_ref[...]
        newkey = newkey_ref[...]
        include = (newkey > t) | ((newkey == t) & (n_chunks * span <= jj))
        s_new = jnp.sum(dq_ref[...].astype(F32) * dkn_ref[...], axis=1, keepdims=True) + bias0_ref[...]
        _row_flash_last(s_new, include, dvn_ref[...], m_ref, l_ref, acc_ref, o_ref)


def _sample_bias(bias_tbl, n_past):
    row = bias_tbl[_rel_bucket(n_past - jnp.arange(n_past))]
    return row.T, bias_tbl[0].reshape(-1, 1)


def _sample_common(page_table):
    db, n_pages = page_table.shape
    pages = min(PAGES_PER_STEP, n_pages)
    assert n_pages % pages == 0
    return db, n_pages, pages, n_pages // pages


def _dsa_sample(sp, caches, layer, page_table, bias_tbl):
    cache_ik, cache_k, cache_v = caches
    db, n_pages, pages, n_chunks = _sample_common(page_table)
    n_past = n_pages * PAGE_SIZE
    span = pages * PAGE_SIZE
    k_sel = min(DSA_TOPK, (n_past + 1) // 4)
    bias, bias0 = _sample_bias(bias_tbl, n_past)
    per_b = lambda *shape: pl.BlockSpec((None,) + shape, lambda b, j, pt: (b,) + (0,) * len(shape))
    const = lambda a: pl.BlockSpec(a.shape, lambda b, j, pt: (0,) * a.ndim)
    in_specs = (_page_specs(64, pages, n_chunks, layer, lambda j: j)
                + _page_specs(64, pages, n_chunks, layer, lambda j: j - n_chunks)
                + _page_specs(64, pages, n_chunks, layer, lambda j: j - n_chunks)
                + [per_b(N_HEADS, 256), per_b(N_HEADS, 1), per_b(1, 64), per_b(N_HEADS, 64), per_b(1, 64), per_b(1, 64),
                   pl.BlockSpec((N_HEADS, span), lambda b, j, pt: (0, jnp.clip(j - n_chunks, 0, n_chunks - 1))),
                   const(bias0)])
    grid_spec = pltpu.PrefetchScalarGridSpec(
        num_scalar_prefetch=1, grid=(db, 2 * n_chunks), in_specs=in_specs,
        out_specs=per_b(N_HEADS, HEAD_DIM),
        scratch_shapes=[pltpu.VMEM((n_chunks, 1, span), I32), pltpu.VMEM((1, 1), I32), pltpu.VMEM((1, 1), I32),
                        pltpu.VMEM((1, 1), I32), pltpu.VMEM((N_HEADS, 1), F32), pltpu.VMEM((N_HEADS, 1), F32),
                        pltpu.VMEM((N_HEADS, HEAD_DIM), F32)])
    return pl.pallas_call(
        functools.partial(_dsa_sample_kernel, k_sel, n_chunks, pages),
        grid_spec=grid_spec,
        out_shape=jax.ShapeDtypeStruct((db, N_HEADS, HEAD_DIM), F32),
        compiler_params=_cparams(("parallel", "arbitrary")),
        name="dsa_sample",
    )(page_table.reshape(-1), *([cache_ik] * pages), *([cache_k] * pages), *([cache_v] * pages),
      sp["iq3"], sp["w"], sp["ik"], sp["dq"], sp["dk"], sp["dv"], bias, bias0)


def _moba_sample_kernel(n_chunks, pages, *refs):
    k1_pages = refs[1:1 + pages]
    k2_pages = refs[1 + pages:1 + 2 * pages]
    v_pages = refs[1 + 2 * pages:1 + 3 * pages]
    (q3_ref, q_ref, kn_ref, vn_ref, bias_ref, bias0_ref, o_ref,
     km_ref, sel_ref, m_ref, l_ref, acc_ref) = refs[1 + 3 * pages:]
    j = pl.program_id(1)
    span = pages * PAGE_SIZE
    blocks = span // MOBA_BLOCK
    nblk = n_chunks * blocks

    @pl.when(j < n_chunks)
    def _():
        k = _cat_pages(k1_pages)
        km_ref[j] = jnp.mean(k.reshape(blocks, MOBA_BLOCK, 128), axis=1)

    @pl.when(j == n_chunks - 1)
    def _():
        km = km_ref[...].reshape(nblk, 128)
        blk = lax.broadcasted_iota(I32, (4, nblk), 1)
        for g in range(MOBA_KV_HEADS):
            km_hi, km_lo = _split_bf16(km[:, g * 64:(g + 1) * 64])
            q_hi = q3_ref[g * 4:(g + 1) * 4, 0:64]
            q_lo = q3_ref[g * 4:(g + 1) * 4, 128:192]
            cur = _dot3(q_hi, q_lo, km_hi, km_lo, _dot_nt)
            sel = jnp.zeros((4, nblk), F32)
            for _ in range(min(MOBA_TOPK, nblk)):
                mx = jnp.max(cur, axis=1, keepdims=True)
                first = jnp.min(jnp.where(cur == mx, blk, nblk), axis=1, keepdims=True)
                pick = (blk == first) & (mx > -jnp.inf)
                sel = jnp.where(pick, 1.0, sel)
                cur = jnp.where(pick, -jnp.inf, cur)
            for c in range(n_chunks):
                sel_ref[c, g * 4:(g + 1) * 4, :] = sel[:, c * blocks:(c + 1) * blocks]
        _flash_init(m_ref, l_ref, acc_ref)

    @pl.when(j >= n_chunks)
    def _():
        c = j - n_chunks
        k = _cat_pages(k2_pages).astype(BF16)
        v = _cat_pages(v_pages).astype(BF16)
        expand = (lax.broadcasted_iota(I32, (blocks, span), 1) // MOBA_BLOCK
                  == lax.broadcasted_iota(I32, (blocks, span), 0)).astype(F32)
        mask_all = _dot(sel_ref[c], expand) > 0.5
        for g in range(MOBA_KV_HEADS):
            rows = slice(g * 4, (g + 1) * 4)
            s = _dot_nt(q_ref[rows, :], k[:, g * 64:(g + 1) * 64]) + bias_ref[rows, :]
            _row_flash_step(s, mask_all[rows], v[:, g * 64:(g + 1) * 64], m_ref, l_ref, acc_ref, rows)

    @pl.when(j == 2 * n_chunks - 1)
    def _():
        q = q_ref[...].astype(F32)
        kn = kn_ref[...]
        vn = vn_ref[...]
        grp = lax.broadcasted_iota(I32, (N_HEADS, 1), 0) >= 4
        s_new = jnp.where(grp, jnp.sum(q * kn[:, 64:], axis=1, keepdims=True),
                          jnp.sum(q * kn[:, :64], axis=1, keepdims=True)) + bias0_ref[...]
        v_new = jnp.where(grp, vn[:, 64:], vn[:, :64])
        _row_flash_last(s_new, True, v_new, m_ref, l_ref, acc_ref, o_ref)


def _moba_sample(sp, caches, layer, page_table, bias_tbl):
    cache_k, cache_v = caches
    db, n_pages, pages, n_chunks = _sample_common(page_table)
    n_past = n_pages * PAGE_SIZE
    span = pages * PAGE_SIZE
    assert span % MOBA_BLOCK == 0
    blocks = span // MOBA_BLOCK
    bias, bias0 = _sample_bias(bias_tbl, n_past)
    per_b = lambda *shape: pl.BlockSpec((None,) + shape, lambda b, j, pt: (b,) + (0,) * len(shape))
    const = lambda a: pl.BlockSpec(a.shape, lambda b, j, pt: (0,) * a.ndim)
    in_specs = (_page_specs(128, pages, n_chunks, layer, lambda j: j)
                + _page_specs(128, pages, n_chunks, layer, lambda j: j - n_chunks)
                + _page_specs(128, pages, n_chunks, layer, lambda j: j - n_chunks)
                + [per_b(N_HEADS, 256), per_b(N_HEADS, 64), per_b(1, 128), per_b(1, 128),
                   pl.BlockSpec((N_HEADS, span), lambda b, j, pt: (0, jnp.clip(j - n_chunks, 0, n_chunks - 1))),
                   const(bias0)])
    grid_spec = pltpu.PrefetchScalarGridSpec(
        num_scalar_prefetch=1, grid=(db, 2 * n_chunks), in_specs=in_specs,
        out_specs=per_b(N_HEADS, HEAD_DIM),
        scratch_shapes=[pltpu.VMEM((n_chunks, blocks, 128), F32), pltpu.VMEM((n_chunks, N_HEADS, blocks), F32),
                        pltpu.VMEM((N_HEADS, 1), F32), pltpu.VMEM((N_HEADS, 1), F32),
                        pltpu.VMEM((N_HEADS, HEAD_DIM), F32)])
    return pl.pallas_call(
        functools.partial(_moba_sample_kernel, n_chunks, pages),
        grid_spec=grid_spec,
        out_shape=jax.ShapeDtypeStruct((db, N_HEADS, HEAD_DIM), F32),
        compiler_params=_cparams(("parallel", "arbitrary")),
        name="moba_sample",
    )(page_table.reshape(-1), *([cache_k] * (2 * pages)), *([cache_v] * pages),
      sp["mq3"], sp["mq"], sp["mk"], sp["mv"], bias, bias0)


def _mla_sample_kernel(n_chunks, pages, *refs):
    c_pages = refs[1:1 + pages]
    r_pages = refs[1 + pages:1 + 2 * pages]
    q_ref, cn_ref, rn_ref, o_ref, m_ref, l_ref, acc_ref = refs[1 + 2 * pages:]
    j = pl.program_id(1)
    q_lat = q_ref[:, 0:MLA_KV_LORA]
    q_rope = q_ref[:, MLA_KV_LORA:MLA_QK]

    @pl.when(j == 0)
    def _():
        _flash_init(m_ref, l_ref, acc_ref)

    ckv = _cat_pages(c_pages).astype(BF16)
    s = _dot_nt(q_lat, ckv) + _dot_nt(q_rope, _cat_pages(r_pages).astype(BF16))
    _row_flash_step(s, None, ckv, m_ref, l_ref, acc_ref)

    @pl.when(j == n_chunks - 1)
    def _():
        s_new = (jnp.sum(q_lat.astype(F32) * cn_ref[...], axis=1, keepdims=True)
                 + jnp.sum(q_rope.astype(F32) * rn_ref[...], axis=1, keepdims=True))
        _row_flash_last(s_new, True, cn_ref[...], m_ref, l_ref, acc_ref, o_ref)


def _mla_sample(sp, caches, layer, page_table):
    cache_c, cache_r = caches
    db, n_pages, pages, n_chunks = _sample_common(page_table)
    per_b = lambda *shape: pl.BlockSpec((None,) + shape, lambda b, j, pt: (b,) + (0,) * len(shape))
    in_specs = (_page_specs(MLA_KV_LORA, pages, n_chunks, layer, lambda j: j)
                + _page_specs(MLA_ROPE, pages, n_chunks, layer, lambda j: j)
                + [per_b(N_HEADS, MLA_QK_PAD), per_b(1, MLA_KV_LORA), per_b(1, MLA_ROPE)])
    grid_spec = pltpu.PrefetchScalarGridSpec(
        num_scalar_prefetch=1, grid=(db, n_chunks), in_specs=in_specs,
        out_specs=per_b(N_HEADS, MLA_KV_LORA),
        scratch_shapes=[pltpu.VMEM((N_HEADS, 1), F32), pltpu.VMEM((N_HEADS, 1), F32),
                        pltpu.VMEM((N_HEADS, MLA_KV_LORA), F32)])
    return pl.pallas_call(
        functools.partial(_mla_sample_kernel, n_chunks, pages),
        grid_spec=grid_spec,
        out_shape=jax.ShapeDtypeStruct((db, N_HEADS, MLA_KV_LORA), F32),
        compiler_params=_cparams(("parallel", "arbitrary")),
        name="mla_sample",
    )(page_table.reshape(-1), *([cache_c] * pages), *([cache_r] * pages), sp["q"], sp["ckv"], sp["kr"])


def _layer_norm(y, g, b):
    mu = jnp.mean(y, axis=-1, keepdims=True)
    d = y - mu
    var = jnp.mean(d * d, axis=-1, keepdims=True)
    return d * lax.rsqrt(var + LN_EPS) * g + b


def _merge_kernel(alpha, x_ref, oa_ref, ob_ref, ol_ref, wg_ref, wuv_ref, wb_ref, wo_ref, g_ref, b_ref, y_ref):
    x = x_ref[...]
    gates = jax.nn.sigmoid(_dot(x.astype(BF16), wg_ref[...]))
    oc = _dot(ol_ref[...].astype(BF16), wuv_ref[...])
    mixed = jnp.zeros_like(x)
    for n, o in enumerate((oa_ref[...], ob_ref[...], oc)):
        mixed = mixed + gates[:, n * D_MODEL:(n + 1) * D_MODEL] * _dot(o.astype(BF16), wb_ref[n])
    y = alpha * x + _dot(mixed.astype(BF16), wo_ref[...])
    y_ref[...] = _layer_norm(y, g_ref[...], b_ref[...])


def _merge(x, o_a, o_b, o_lat, mw, alpha, tm):
    n = x.shape[0]
    w_gate, wuv_bd, w_branch, w_out, ln_g, ln_b = mw
    tok = lambda w: pl.BlockSpec((tm, w), lambda i: (i, 0))
    const = lambda a: pl.BlockSpec(a.shape, lambda i: (0,) * a.ndim)
    consts = (w_gate, wuv_bd, w_branch, w_out, ln_g, ln_b)
    return pl.pallas_call(
        functools.partial(_merge_kernel, alpha),
        grid=(n // tm,),
        in_specs=[tok(D_MODEL), tok(BRANCH_WIDTH), tok(BRANCH_WIDTH), tok(N_HEADS * MLA_KV_LORA)]
        + [const(a) for a in consts],
        out_specs=tok(D_MODEL),
        out_shape=jax.ShapeDtypeStruct((n, D_MODEL), F32),
        compiler_params=_cparams(("parallel",)),
        name="merge",
    )(x, o_a, o_b, o_lat, *consts)


def _prep_merge_weights(w_in, w_uv, w_branch, w_out, ln_g, ln_b):
    w_gate = w_in[:, IN_OFFS[12]:IN_OFFS[13]].astype(BF16)
    eye = jnp.eye(N_HEADS, dtype=F32)
    wuv_bd = (eye[:, None, :, None] * w_uv[:, :, None, :]).reshape(N_HEADS * MLA_KV_LORA, BRANCH_WIDTH)
    return (w_gate, wuv_bd.astype(BF16), w_branch.astype(BF16), w_out.astype(BF16),
            ln_g.reshape(1, -1), ln_b.reshape(1, -1))


def _extract_top(cur, count, out_ref):
    rows = cur.shape[0]
    iota = lax.broadcasted_iota(I32, cur.shape, 0)
    for r in range(count):
        mx = jnp.max(cur, axis=0, keepdims=True)
        first = jnp.min(jnp.where(cur == mx, iota, rows), axis=0, keepdims=True)
        cur = jnp.where(iota == first, -jnp.inf, cur)
        out_ref[r:r + 1, :] = mx


def _peer_select_kernel(x_ref, wq_hi_ref, wq_lo_ref, k_hi_ref, k_lo_ref,
                        s_ref, e_ref, t_ref, v0_ref, v1_ref, cand_ref, best_ref):
    x_hi, x_lo = _split_bf16(x_ref[...])
    qt = _dot3(wq_hi_ref[...], wq_lo_ref[...], x_hi, x_lo, _dot_nt)
    half = PEER_KEY_DIM // 2
    for hd in range(PEER_HEADS):
        for p, v_ref in ((0, v0_ref), (1, v1_ref)):
            r0 = (hd * 2 + p) * half
            q_hi, q_lo = _split_bf16(qt[r0:r0 + half])
            s = _dot3(k_hi_ref[p, hd], k_lo_ref[p, hd], q_hi, q_lo, _dot)
            s_ref[p, hd] = s
            _extract_top(s, PEER_TOPK, v_ref)
        v1 = v1_ref[...]
        for a in range(PEER_TOPK):
            cand_ref[a * PEER_TOPK:(a + 1) * PEER_TOPK, :] = v0_ref[a:a + 1, :] + v1
        _extract_top(cand_ref[...], PEER_TOPK, best_ref)
        best = best_ref[...]
        top = best[0:1, :]
        z = jnp.sum(jnp.exp(best - top), axis=0, keepdims=True)
        t_ref[hd:hd + 1, :] = best[PEER_TOPK - 1:PEER_TOPK, :]
        e_ref[0, hd] = jnp.exp(s_ref[0, hd] - v0_ref[0:1, :])
        e_ref[1, hd] = jnp.exp(s_ref[1, hd] - v1_ref[0:1, :]) / z


def _peer_select(x, pw, tm):
    n = x.shape[0]
    wq_hi, wq_lo, k_hi, k_lo = pw[:4]
    const = lambda a: pl.BlockSpec(a.shape, lambda i: (0,) * a.ndim)
    sblock = pl.BlockSpec((2, PEER_HEADS, PEER_NKEYS, tm), lambda i: (0, 0, 0, i))
    sshape = jax.ShapeDtypeStruct((2, PEER_HEADS, PEER_NKEYS, n), F32)
    return pl.pallas_call(
        _peer_select_kernel,
        grid=(n // tm,),
        in_specs=[pl.BlockSpec((tm, D_MODEL), lambda i: (i, 0))] + [const(a) for a in (wq_hi, wq_lo, k_hi, k_lo)],
        out_specs=[sblock, sblock, pl.BlockSpec((PEER_HEADS, tm), lambda i: (0, i))],
        out_shape=[sshape, sshape, jax.ShapeDtypeStruct((PEER_HEADS, n), F32)],
        scratch_shapes=[pltpu.VMEM((PEER_TOPK, tm), F32), pltpu.VMEM((PEER_TOPK, tm), F32),
                        pltpu.VMEM((PEER_TOPK * PEER_TOPK, tm), F32), pltpu.VMEM((PEER_TOPK, tm), F32)],
        compiler_params=_cparams(("parallel",)),
        name="peer_select",
    )(x, wq_hi, wq_lo, k_hi, k_lo)


def _gelu_tanh(x):
    return 0.5 * x * (1.0 + jnp.tanh(math.sqrt(2.0 / math.pi) * (x + 0.044715 * (x * x * x))))


def _peer_dense_kernel(alpha, x_ref, s_ref, e_ref, t_ref, u_ref, vt_ref, g_ref, b_ref, y_ref, acc_ref):
    c = pl.program_id(1)
    rows_per_step = u_ref.shape[0] // PEER_NKEYS

    @pl.when(c == 0)
    def _():
        acc_ref[...] = jnp.zeros(acc_ref.shape, F32)

    act = _gelu_tanh(_dot_nt(u_ref[...], x_ref[...].astype(BF16)))
    parts = []
    for ii in range(rows_per_step):
        i = c * rows_per_step + ii
        w = jnp.zeros((PEER_NKEYS, act.shape[1]), F32)
        for hd in range(PEER_HEADS):
            pair = s_ref[0, hd, pl.ds(i, 1), :] + s_ref[1, hd]
            gate = e_ref[0, hd, pl.ds(i, 1), :] * e_ref[1, hd]
            w = w + jnp.where(pair >= t_ref[hd:hd + 1, :], gate, 0.0)
        parts.append((act[ii * PEER_NKEYS:(ii + 1) * PEER_NKEYS] * w).astype(BF16))
    acc_ref[...] += _dot(vt_ref[...], jnp.concatenate(parts, axis=0))

    @pl.when(c == pl.num_programs(1) - 1)
    def _():
        y_ref[...] = _layer_norm(alpha * x_ref[...] + acc_ref[...].T, g_ref[...], b_ref[...])


def _peer_dense(x, s, e, t, pw, alpha, tm, ec):
    n = x.shape[0]
    u_bf, vt_bf, ln_g, ln_b = pw[4:]
    n_exp = u_bf.shape[0]
    sblock = pl.BlockSpec((2, PEER_HEADS, PEER_NKEYS, tm), lambda i, c: (0, 0, 0, i))
    const = lambda a: pl.BlockSpec(a.shape, lambda i, c: (0,) * a.ndim)
    return pl.pallas_call(
        functools.partial(_peer_dense_kernel, alpha),
        grid=(n // tm, n_exp // ec),
        in_specs=[pl.BlockSpec((tm, D_MODEL), lambda i, c: (i, 0)), sblock, sblock,
                  pl.BlockSpec((PEER_HEADS, tm), lambda i, c: (0, i)),
                  pl.BlockSpec((ec, D_MODEL), lambda i, c: (c, 0)),
                  pl.BlockSpec((D_MODEL, ec), lambda i, c: (0, c)),
                  const(ln_g), const(ln_b)],
        out_specs=pl.BlockSpec((tm, D_MODEL), lambda i, c: (i, 0)),
        out_shape=jax.ShapeDtypeStruct((n, D_MODEL), F32),
        scratch_shapes=[pltpu.VMEM((D_MODEL, tm), F32)],
        compiler_params=_cparams(("parallel", "arbitrary")),
        name="peer_dense",
    )(x, s, e, t, u_bf, vt_bf, ln_g, ln_b)


def _prep_peer_weights(peer_wq, peer_keys, peer_u, peer_v, ln_g, ln_b):
    wq_hi, wq_lo = _split_bf16(peer_wq.T)
    k_hi, k_lo = _split_bf16(peer_keys)
    return (wq_hi, wq_lo, k_hi, k_lo, peer_u.astype(BF16), peer_v.T.astype(BF16),
            ln_g.reshape(1, -1), ln_b.reshape(1, -1))


PROMPT_TILE = ATT_TILE
PEER_TOKEN_TILE = 512
PEER_EXPERT_CHUNK = 512


def _channel_mix(x1, peer_w, alpha):
    n = x1.shape[0]
    tm = min(PROMPT_TILE, n)
    s, e, t = _peer_select(x1, peer_w, tm)
    return _peer_dense(x1, s, e, t, peer_w, alpha, min(PEER_TOKEN_TILE, n), PEER_EXPERT_CHUNK)


def _sample_views(q):
    db = q["dk"].shape[1]
    heads_major = lambda a: jnp.transpose(a[0], (2, 0, 1))
    rows = lambda a: a[0][:, None, :]
    return {
        "iq3": heads_major(q["iq3t"]), "w": jnp.transpose(q["wt"][0])[:, :, None], "ik": rows(q["ik"]),
        "dq": heads_major(q["dqt"].reshape(1, N_HEADS, HEAD_DIM, db)), "dk": rows(q["dk"]), "dv": rows(q["dv"]),
        "mq3": heads_major(q["mq3t"]), "mq": heads_major(q["mqt"].reshape(1, N_HEADS, HEAD_DIM, db)),
        "mk": rows(q["mk"]), "mv": rows(q["mv"]),
        "q": heads_major(q["qt"]), "ckv": rows(q["ckv"]), "kr": rows(q["kr"]),
    }


def kernel(x_prompt, x_sample, cache_dsa_k, cache_dsa_v, cache_dsa_idx_k, cache_moba_k, cache_moba_v, cache_mla_ckv, cache_mla_krope, page_table, rel_bias, w_in, w_uq, g_cq, g_ckv, w_uk, w_uv, w_branch, w_out, ln1_g, ln1_b, peer_wq, peer_keys, peer_u, peer_v, ln2_g, ln2_b):
    b, t, _ = x_prompt.shape
    db, dec_seq, _ = x_sample.shape
    assert dec_seq == 1 and t % ATT_TILE == 0
    depth = w_in.shape[0]
    n_pages = page_table.shape[1]
    n_past = n_pages * PAGE_SIZE
    assert n_past % MOBA_BLOCK == 0
    alpha = (2 * depth) ** 0.25
    pos_p = jnp.broadcast_to(jnp.arange(t, dtype=F32), (b, t))
    pos_s = jnp.full((1, db), n_past, F32)
    pool = cache_moba_k.shape[:3]
    moba_k = cache_moba_k.reshape(pool + (MOBA_KV_HEADS * HEAD_DIM,))
    moba_v = cache_moba_v.reshape(pool + (MOBA_KV_HEADS * HEAD_DIM,))
    bias_a, bias_b = rel_bias[:, :N_HEADS], rel_bias[:, N_HEADS:]

    hp = x_prompt.reshape(b * t, D_MODEL)
    hs = x_sample.reshape(db, D_MODEL)
    rows_p, rows_s = [], []
    for l in range(depth):
        proj_w = _prep_proj_weights(w_in[l], w_uq[l], w_uk[l])
        merge_w = _prep_merge_weights(w_in[l], w_uv[l], w_branch[l], w_out[l], ln1_g[l], ln1_b[l])
        peer_w = _prep_peer_weights(peer_wq[l], peer_keys[l], peer_u[l], peer_v[l], ln2_g[l], ln2_b[l])

        p = _project(hp.reshape(b, t, D_MODEL), pos_p, proj_w, g_cq[l], g_ckv[l], PROMPT_TILE)
        flat = lambda a: a.reshape(b * t, a.shape[-1])
        x1 = _merge(hp, flat(_dsa_prompt(p, bias_a)), flat(_moba_prompt(p, bias_b)), flat(_mla_prompt(p)),
                    merge_w, alpha, PROMPT_TILE)
        hp = _channel_mix(x1, peer_w, alpha)
        rows_p.append(p)

        q = _project(hs.reshape(1, db, D_MODEL), pos_s, proj_w, g_cq[l], g_ckv[l], db)
        sp = _sample_views(q)
        o_a = _dsa_sample(sp, (cache_dsa_idx_k, cache_dsa_k, cache_dsa_v), l, page_table, bias_a)
        o_b = _moba_sample(sp, (moba_k, moba_v), l, page_table, bias_b)
        o_l = _mla_sample(sp, (cache_mla_ckv, cache_mla_krope), l, page_table)
        x1 = _merge(hs, o_a.reshape(db, -1), o_b.reshape(db, -1), o_l.reshape(db, -1), merge_w, alpha, db)
        hs = _channel_mix(x1, peer_w, alpha)
        rows_s.append(q)

    def stack(name, tail):
        new_p = jnp.stack([r[name].reshape((b, t) + tail) for r in rows_p])
        new_s = jnp.stack([r[name].reshape((db, 1) + tail) for r in rows_s])
        return new_p, new_s

    outs = [hp.reshape(b, t, D_MODEL), hs.reshape(db, 1, D_MODEL)]
    for name, tail in (("dk", (HEAD_DIM,)), ("dv", (HEAD_DIM,)), ("ik", (DSA_IDX_DIM,)),
                       ("mk", (MOBA_KV_HEADS, HEAD_DIM)), ("mv", (MOBA_KV_HEADS, HEAD_DIM)),
                       ("ckv", (MLA_KV_LORA,)), ("kr", (MLA_ROPE,))):
        outs.extend(stack(name, tail))
    return tuple(outs)
```

```python
import functools
import math

import jax
import jax.numpy as jnp
import numpy as np
from jax import lax
from jax.experimental import pallas as pl
from jax.experimental.pallas import tpu as pltpu

F32 = jnp.float32
BF16 = jnp.bfloat16
I32 = jnp.int32

D_MODEL = 1024
PAGE_SIZE = 128
HEAD_DIM = 64
N_HEADS = 8
DSA_IDX_DIM = 64
DSA_TOPK = 256
MOBA_KV_HEADS = 2
MOBA_BLOCK = 256
MOBA_TOPK = 3
MLA_Q_LORA = 256
MLA_KV_LORA = 128
MLA_NOPE = 64
MLA_ROPE = 32
MLA_QK = MLA_KV_LORA + MLA_ROPE
MLA_QK_PAD = 256
ROPE_THETA = 10000.0
N_BRANCH = 3
BRANCH_WIDTH = 512
REL_BUCKETS = 32
REL_MAX_DIST = 128
PEER_HEADS = 8
PEER_NKEYS = 128
PEER_KEY_DIM = 256
PEER_TOPK = 16
LN_EPS = 1e-5
RMS_EPS = 1e-6
NEG = -1e30

VMEM_LIMIT_BYTES = 56 * 1024 * 1024
ATT_TILE = 256

IN_WIDTHS = (512, 64, 64, 512, 64, 8, 512, 128, 128, 256, 128, 32, 3072)
IN_OFFS = tuple(int(v) for v in np.cumsum((0,) + IN_WIDTHS))

T_DQ, T_IQ, T_MQ, T_CQ, T_CKV, T_MV, T_DV, T_IW, T_ROWS = 0, 512, 1024, 1536, 1792, 1920, 2048, 2112, 2128
N_MK, N_MV, N_CKV, N_DK, N_DV, N_IK, N_KR, N_KRS, N_COLS = 0, 128, 256, 384, 512, 640, 768, 896, 1024


def _split_bf16(a):
    hi = a.astype(BF16)
    lo = (a - hi.astype(F32)).astype(BF16)
    return hi, lo


def _dot(a, b):
    return jnp.dot(a, b, preferred_element_type=F32)


def _dot_nt(a, b):
    return lax.dot_general(a, b, (((1,), (1,)), ((), ())), preferred_element_type=F32)


def _dot3(a_hi, a_lo, b_hi, b_lo, dot):
    return dot(a_hi, b_hi) + dot(a_hi, b_lo) + dot(a_lo, b_hi)


def _cparams(sem, vmem=VMEM_LIMIT_BYTES):
    return pltpu.CompilerParams(dimension_semantics=sem, vmem_limit_bytes=vmem)


def _proj_kernel(x_ref, posr_ref, posc_ref, wn_hi_ref, wn_lo_ref, wt_hi_ref, wt_lo_ref,
                 gcq_ref, gckv_c_ref, gckv_r_ref, wuq_ref, wuk_ref, fcol_ref, frow_ref, srow_ref,
                 mk_ref, mv_ref, ckv_ref, dk_ref, dv_ref, ik_ref, kr_ref,
                 mkb_ref, dkb_ref, ik3_ref, kc_ref, km_ref,
                 dqt_ref, iq3t_ref, mqt_ref, mq3t_ref, qt_ref, ckvt_ref, mvt_ref, dvt_ref, wt_ref):
    x = x_ref[...]
    x_hi, x_lo = _split_bf16(x)
    h = _dot3(x_hi, x_lo, wn_hi_ref[...], wn_lo_ref[...], _dot)
    ht = _dot3(wt_hi_ref[...], wt_lo_ref[...], x_hi, x_lo, _dot_nt)

    mk = h[:, N_MK:N_MK + 128]
    mk_ref[...] = mk
    mkb_ref[0] = mk[:, :64].astype(BF16)
    mkb_ref[1] = mk[:, 64:].astype(BF16)
    km_ref[...] = jnp.mean(mk, axis=0, keepdims=True)
    mv_ref[...] = h[:, N_MV:N_MV + 128]
    c = h[:, N_CKV:N_CKV + 128]
    ckv = c * lax.rsqrt(jnp.mean(c * c, axis=-1, keepdims=True) + RMS_EPS) * gckv_r_ref[...]
    ckv_ref[...] = ckv
    dk = h[:, N_DK:N_DK + 64]
    dk_ref[...] = dk
    dkb_ref[...] = dk.astype(BF16)
    dv_ref[...] = h[:, N_DV:N_DV + 64]
    ik2 = h[:, N_IK:N_IK + 128]
    ik_ref[...] = ik2[:, :64]
    ik_hi = ik2.astype(BF16)
    ik_lo = (ik2 - ik_hi.astype(F32)).astype(BF16)
    left = lax.broadcasted_iota(I32, ik2.shape, 1) < 64
    zero = jnp.zeros_like(ik_hi)
    ik3_ref[...] = jnp.concatenate([jnp.where(left, ik_hi, ik_lo), jnp.where(left, ik_hi, zero)], axis=1)
    ang = posc_ref[...] * frow_ref[...]
    krr = h[:, N_KR:N_KR + 128] * jnp.cos(ang) + h[:, N_KRS:N_KRS + 128] * (jnp.sin(ang) * srow_ref[...])
    kr_ref[...] = krr[:, :MLA_ROPE]
    kc_ref[...] = jnp.concatenate([ckv, krr], axis=1).astype(BF16)

    tm = x.shape[0]
    mq = ht[T_MQ:T_MQ + 512]
    iq = ht[T_IQ:T_IQ + 512]
    for hd in range(N_HEADS):
        cols = slice(hd * tm, (hd + 1) * tm)
        rows = slice(hd * 64, (hd + 1) * 64)
        dqt_ref[:, cols] = (ht[T_DQ:T_DQ + 512][rows] * (HEAD_DIM ** -0.5)).astype(BF16)
        mqt_ref[:, cols] = (mq[rows] * (HEAD_DIM ** -0.5)).astype(BF16)
        for src, dst in ((iq, iq3t_ref), (mq, mq3t_ref)):
            q_hi, q_lo = _split_bf16(src[rows])
            dst[:, cols] = jnp.concatenate([q_hi, q_hi, q_lo, jnp.zeros_like(q_hi)], axis=0)
        wt_ref[:, cols] = ht[T_IW + hd:T_IW + hd + 1] * (N_HEADS ** -0.5) * (DSA_IDX_DIM ** -0.5)
    mvt_ref[...] = ht[T_MV:T_MV + 128].astype(BF16)
    dvt_ref[...] = ht[T_DV:T_DV + 64].astype(BF16)
    ct = ht[T_CKV:T_CKV + 128]
    ckvt = ct * lax.rsqrt(jnp.mean(ct * ct, axis=0, keepdims=True) + RMS_EPS) * gckv_c_ref[...]
    ckvt_ref[...] = ckvt.astype(BF16)
    cq = ht[T_CQ:T_CQ + 256]
    cqn = cq * lax.rsqrt(jnp.mean(cq * cq, axis=0, keepdims=True) + RMS_EPS) * gcq_ref[...]
    qc = _dot(wuq_ref[...], cqn.astype(BF16))
    angt = fcol_ref[...] * posr_ref[...]
    cs, sn = jnp.cos(angt), jnp.sin(angt)
    scale = (MLA_NOPE + MLA_ROPE) ** -0.5
    half = MLA_ROPE // 2
    for hd in range(N_HEADS):
        base = hd * (MLA_NOPE + MLA_ROPE)
        nope = qc[base:base + MLA_NOPE]
        r1 = qc[base + MLA_NOPE:base + MLA_NOPE + half]
        r2 = qc[base + MLA_NOPE + half:base + MLA_NOPE + MLA_ROPE]
        ql = _dot(wuk_ref[hd], nope.astype(BF16))
        qrows = jnp.concatenate([ql, r1 * cs - r2 * sn, r2 * cs + r1 * sn], axis=0) * scale
        qt_ref[:, hd * tm:(hd + 1) * tm] = jnp.concatenate(
            [qrows.astype(BF16), jnp.zeros((MLA_QK_PAD - MLA_QK, tm), BF16)], axis=0)


def _rope_tables():
    half = MLA_ROPE // 2
    freq = ROPE_THETA ** (-jnp.arange(half, dtype=F32) / half)
    frow = jnp.zeros((1, 128), F32).at[0, :MLA_ROPE].set(jnp.concatenate([freq, freq]))
    srow = jnp.zeros((1, 128), F32).at[0, :MLA_ROPE].set(
        jnp.concatenate([-jnp.ones((half,), F32), jnp.ones((half,), F32)]))
    return freq.reshape(half, 1), frow, srow


def _prep_proj_weights(w_in, w_uq, w_uk):
    def col(i):
        return w_in[:, IN_OFFS[i]:IN_OFFS[i + 1]]

    dq, dk, dv, iq, ik, iw, mq, mk, mv, cq, ckv, kr = (col(i) for i in range(12))
    z = lambda n: jnp.zeros((D_MODEL, n), F32)
    half = MLA_ROPE // 2
    kr_sw = jnp.concatenate([kr[:, half:], kr[:, :half]], axis=1)
    wn = jnp.concatenate([mk, mv, ckv, dk, z(64), dv, z(64), ik, ik, kr, z(96), kr_sw, z(96)], axis=1)
    wt = jnp.concatenate([dq, iq, mq, cq, ckv, mv, dv, iw, z(T_ROWS - T_IW - 8)], axis=1).T
    wn_hi, wn_lo = _split_bf16(wn)
    wt_hi, wt_lo = _split_bf16(wt)
    return wn_hi, wn_lo, wt_hi, wt_lo, w_uq.T.astype(BF16), w_uk.astype(BF16)


def _project(x, pos, pw, g_cq, g_ckv, tm):
    b, t, _ = x.shape
    wn_hi, wn_lo, wt_hi, wt_lo, wuq_t, wuk = pw
    fcol, frow, srow = _rope_tables()
    nt = t // tm
    grid = (b, nt)
    tok = lambda w: pl.BlockSpec((None, tm, w), lambda bi, i: (bi, i, 0))
    feat = lambda r: pl.BlockSpec((None, r, tm), lambda bi, i: (bi, 0, i))
    feat_c = lambda r: pl.BlockSpec((None, None, r, tm), lambda bi, i: (bi, i, 0, 0))
    feat_w = lambda r: pl.BlockSpec((None, None, r, N_HEADS * tm), lambda bi, i: (bi, i, 0, 0))
    full = lambda a: pl.BlockSpec(a.shape, lambda bi, i: (0,) * a.ndim)
    sds = jax.ShapeDtypeStruct
    wide = lambda r, dt: sds((b, nt, r, N_HEADS * tm), dt)
    consts = (wn_hi, wn_lo, wt_hi, wt_lo, g_cq.reshape(-1, 1), g_ckv.reshape(-1, 1), g_ckv.reshape(1, -1),
              wuq_t, wuk, fcol, frow, srow)
    out_specs = [tok(128), tok(128), tok(128), tok(64), tok(64), tok(64), tok(MLA_ROPE),
                 pl.BlockSpec((None, 2, tm, 64), lambda bi, i: (bi, 0, i, 0)), tok(64), tok(256), tok(256),
                 pl.BlockSpec((None, None, 1, 128), lambda bi, i: (bi, i, 0, 0)),
                 feat_w(64), feat_w(256), feat_w(64), feat_w(256), feat_w(MLA_QK_PAD), feat_c(128), feat_c(128),
                 feat_c(64), feat_w(1)]
    out_shape = [sds((b, t, 128), F32), sds((b, t, 128), F32), sds((b, t, 128), F32), sds((b, t, 64), F32),
                 sds((b, t, 64), F32), sds((b, t, 64), F32), sds((b, t, MLA_ROPE), F32),
                 sds((b, 2, t, 64), BF16), sds((b, t, 64), BF16), sds((b, t, 256), BF16), sds((b, t, 256), BF16),
                 sds((b, nt, 1, 128), F32),
                 wide(64, BF16), wide(256, BF16), wide(64, BF16), wide(256, BF16), wide(MLA_QK_PAD, BF16),
                 sds((b, nt, 128, tm), BF16), sds((b, nt, 128, tm), BF16), sds((b, nt, 64, tm), BF16), wide(1, F32)]
    outs = pl.pallas_call(
        _proj_kernel,
        grid=grid,
        in_specs=[tok(D_MODEL), pl.BlockSpec((None, 1, tm), lambda bi, i: (bi, 0, i)), tok(1)]
        + [full(a) for a in consts],
        out_specs=out_specs,
        out_shape=out_shape,
        compiler_params=_cparams(("parallel", "parallel")),
        name="proj",
    )(x, pos.reshape(b, 1, t), pos.reshape(b, t, 1), *consts)
    names = ("mk", "mv", "ckv", "dk", "dv", "ik", "kr", "mkb", "dkb", "ik3", "kc", "km",
             "dqt", "iq3t", "mqt", "mq3t", "qt", "ckvt", "mvt", "dvt", "wt")
    return dict(zip(names, outs))


def _rel_bucket(dist):
    n = jnp.maximum(dist, 0)
    exact = REL_BUCKETS // 2
    big = exact + (jnp.log(jnp.maximum(n, 1).astype(F32) / exact) / math.log(REL_MAX_DIST / exact)
                   * (REL_BUCKETS - exact)).astype(I32)
    return jnp.where(n < exact, n, jnp.minimum(big, REL_BUCKETS - 1))


def _bias_tiles(bias_tbl, tile):
    kk = jnp.arange(tile)[:, None]
    qq = jnp.arange(tile)[None, :]
    d0 = jnp.where((kk <= qq)[:, :, None], bias_tbl[_rel_bucket(qq - kk)], NEG)
    d1 = bias_tbl[_rel_bucket(tile + qq - kk)]
    far = bias_tbl[_rel_bucket(jnp.full((tile,), 2 * tile))]
    mv = lambda a: jnp.moveaxis(a, -1, 0)
    return mv(d0), mv(d1), far.T.reshape(1, -1)


def _flash_init(m_ref, l_ref, acc_ref):
    m_ref[...] = jnp.full(m_ref.shape, NEG, F32)
    l_ref[...] = jnp.zeros(l_ref.shape, F32)
    acc_ref[...] = jnp.zeros(acc_ref.shape, F32)


def _flash_update(s_parts, vts, tile, m_ref, l_ref, acc_ref, tile_bias=None, col_off=None, add_mask=None):
    m_old, l_old = m_ref[...], l_ref[...]
    m_cols, l_cols, a_cols, pv = [], [], [], []
    hd = 0
    for s_all, vt in zip(s_parts, vts):
        p_cols = []
        for j in range(s_all.shape[1] // tile):
            cols = slice(hd * tile, (hd + 1) * tile)
            s = s_all[:, j * tile:(j + 1) * tile]
            if tile_bias is not None:
                s = s + tile_bias[hd]
            if add_mask is not None:
                s = s + add_mask
            mx = jnp.max(s, axis=0, keepdims=True)
            if col_off is not None:
                c = col_off[:, cols]
                m_new = jnp.maximum(m_old[:, cols], mx + c)
                off = jnp.maximum(m_new - c, mx)
            else:
                m_new = jnp.maximum(m_old[:, cols], mx)
                off = m_new
            p = jnp.exp(s - off)
            alpha = jnp.exp(m_old[:, cols] - m_new)
            l_cols.append(l_old[:, cols] * alpha + jnp.sum(p, axis=0, keepdims=True))
            m_cols.append(m_new)
            a_cols.append(alpha)
            p_cols.append(p.astype(BF16))
            hd += 1
        pv.append(_dot(vt, jnp.concatenate(p_cols, axis=1)))
    m_ref[...] = jnp.concatenate(m_cols, axis=1)
    l_ref[...] = jnp.concatenate(l_cols, axis=1)
    acc_ref[...] = acc_ref[...] * jnp.concatenate(a_cols, axis=1) + jnp.concatenate(pv, axis=1)


def _flash_finish(o_ref, l_ref, acc_ref, tile):
    width = acc_ref.shape[0]
    o = acc_ref[...] / l_ref[...]
    for hd in range(N_HEADS):
        o_ref[:, hd * width:(hd + 1) * width] = o[:, hd * tile:(hd + 1) * tile].T


def _causal_mask(tile):
    return lax.broadcasted_iota(I32, (tile, tile), 0) <= lax.broadcasted_iota(I32, (tile, tile), 1)


def _chunk_rows(c, tile):
    return pl.ds(pl.multiple_of(c * tile, tile), tile)


def _wide_spec(rows, tile):
    return pl.BlockSpec((None, None, rows, N_HEADS * tile), lambda bi, i: (bi, i, 0, 0))


def _flash_scratch(dv, tile):
    return [pltpu.VMEM((1, N_HEADS * tile), F32), pltpu.VMEM((1, N_HEADS * tile), F32),
            pltpu.VMEM((dv, N_HEADS * tile), F32)]


def _mla_prompt_kernel(qt_ref, kc_ref, vt_ref, o_ref, m_ref, l_ref, acc_ref):
    i = pl.program_id(1)
    tile = o_ref.shape[0]
    _flash_init(m_ref, l_ref, acc_ref)

    def chunk(c, add_mask):
        s = _dot(kc_ref[_chunk_rows(c, tile), :], qt_ref[...])
        _flash_update([s], [vt_ref[c]], tile, m_ref, l_ref, acc_ref, add_mask=add_mask)

    def body(c, carry):
        chunk(c, None)
        return carry

    lax.fori_loop(0, i, body, 0)
    chunk(i, jnp.where(_causal_mask(tile), 0.0, NEG))
    _flash_finish(o_ref, l_ref, acc_ref, tile)


def _mla_prompt(p):
    b, t, _ = p["kc"].shape
    tile = ATT_TILE
    nc = t // tile
    return pl.pallas_call(
        _mla_prompt_kernel,
        grid=(b, nc),
        in_specs=[_wide_spec(MLA_QK_PAD, tile),
                  pl.BlockSpec((None, t, 256), lambda bi, i: (bi, 0, 0)),
                  pl.BlockSpec((None, nc, MLA_KV_LORA, tile), lambda bi, i: (bi, 0, 0, 0))],
        out_specs=pl.BlockSpec((None, tile, N_HEADS * MLA_KV_LORA), lambda bi, i: (bi, i, 0)),
        out_shape=jax.ShapeDtypeStruct((b, t, N_HEADS * MLA_KV_LORA), F32),
        scratch_shapes=_flash_scratch(MLA_KV_LORA, tile),
        compiler_params=_cparams(("parallel", "parallel")),
        name="mla_prompt",
    )(p["qt"], p["kc"], p["ckvt"])


INT_MIN = -2 ** 31
INT_MAX = 2 ** 31 - 1


def _order_key(s):
    bits = pltpu.bitcast(jnp.where(s == 0.0, 0.0, s), I32)
    return jnp.where(bits < 0, bits ^ INT_MAX, bits)


def _topk_threshold(count_ge, count_gt_eq, count_eq_below, k, lanes, idx_bits):
    def bit_step(n, t):
        cand = t ^ jnp.left_shift(jnp.int32(1), 31 - n)
        return jnp.where(count_ge(cand) >= k, cand, t)

    t = lax.fori_loop(0, 32, bit_step, jnp.full((1, lanes), INT_MIN, I32))
    n_gt, n_eq = count_gt_eq(t)
    need = k - n_gt
    excess = (n_eq > need) & (t != INT_MIN)

    def resolve():
        def idx_step(n, j):
            cand = j | jnp.left_shift(jnp.int32(1), idx_bits - 1 - n)
            return jnp.where(count_eq_below(t, cand) < need, cand, j)

        j = lax.fori_loop(0, idx_bits, idx_step, jnp.zeros((1, lanes), I32))
        return jnp.where(excess, j, INT_MAX)

    j = lax.cond(jnp.any(excess), resolve, lambda: jnp.full((1, lanes), INT_MAX, I32))
    return t, j


def _dsa_prompt_kernel(iq3_ref, w_ref, ik3_ref, qt_ref, k_ref, vt_ref, d0_ref, d1_ref, far_ref, o_ref,
                       key_ref, m_ref, l_ref, acc_ref):
    i = pl.program_id(1)
    tile = o_ref.shape[0]
    _flash_init(m_ref, l_ref, acc_ref)
    causal = _causal_mask(tile)
    rows = lambda c: _chunk_rows(c, tile)

    def score(c, carry):
        weighted = jnp.maximum(_dot(ik3_ref[rows(c), :], iq3_ref[...]), 0.0) * w_ref[...]
        s = weighted[:, 0:tile]
        for hd in range(1, N_HEADS):
            s = s + weighted[:, hd * tile:(hd + 1) * tile]
        key = _order_key(s)
        key_ref[c] = jnp.where((c < i) | causal, key, INT_MIN)
        return carry

    lax.fori_loop(0, i + 1, score, 0)

    def count(pred):
        def body(c, acc):
            return acc + jnp.sum(pred(key_ref[c], c).astype(I32), axis=0, keepdims=True)
        return lax.fori_loop(0, i + 1, body, jnp.zeros((1, tile), I32))

    def count_gt_eq(t):
        def body(c, acc):
            key = key_ref[c]
            return (acc[0] + jnp.sum((key > t).astype(I32), axis=0, keepdims=True),
                    acc[1] + jnp.sum((key == t).astype(I32), axis=0, keepdims=True))
        z = jnp.zeros((1, tile), I32)
        return lax.fori_loop(0, i + 1, body, (z, z))

    kidx = lambda c: c * tile + lax.broadcasted_iota(I32, (tile, tile), 0)
    nbits = max(1, (key_ref.shape[0] * tile - 1).bit_length())
    t, j = _topk_threshold(
        lambda cand: count(lambda key, c: key >= cand),
        count_gt_eq,
        lambda tt, cand: count(lambda key, c: (key == tt) & (kidx(c) < cand)),
        DSA_TOPK, tile, nbits)

    def chunk(c, kind):
        key = key_ref[c]
        selected = (key > t) | ((key == t) & (kidx(c) <= j))
        add_mask = jnp.where(selected, 0.0, NEG)
        s = _dot(k_ref[rows(c), :], qt_ref[...])
        bias = {"diag": dict(tile_bias=d0_ref), "prev": dict(tile_bias=d1_ref), "far": dict(col_off=far_ref[...])}[kind]
        _flash_update([s], [vt_ref[c]], tile, m_ref, l_ref, acc_ref, add_mask=add_mask, **bias)

    def body(c, carry):
        chunk(c, "far")
        return carry

    lax.fori_loop(0, jnp.maximum(i - 1, 0), body, 0)

    @pl.when(i >= 1)
    def _():
        chunk(i - 1, "prev")

    chunk(i, "diag")
    _flash_finish(o_ref, l_ref, acc_ref, tile)


def _dsa_prompt(p, bias_tbl):
    b, t, _ = p["dk"].shape
    tile = ATT_TILE
    nc = t // tile
    d0, d1, far = _bias_tiles(bias_tbl, tile)
    const = lambda a: pl.BlockSpec(a.shape, lambda bi, i: (0,) * a.ndim)
    return pl.pallas_call(
        _dsa_prompt_kernel,
        grid=(b, nc),
        in_specs=[_wide_spec(256, tile), _wide_spec(1, tile),
                  pl.BlockSpec((None, t, 256), lambda bi, i: (bi, 0, 0)),
                  _wide_spec(HEAD_DIM, tile),
                  pl.BlockSpec((None, t, 64), lambda bi, i: (bi, 0, 0)),
                  pl.BlockSpec((None, nc, 64, tile), lambda bi, i: (bi, 0, 0, 0)),
                  const(d0), const(d1), const(far)],
        out_specs=pl.BlockSpec((None, tile, N_HEADS * HEAD_DIM), lambda bi, i: (bi, i, 0)),
        out_shape=jax.ShapeDtypeStruct((b, t, N_HEADS * HEAD_DIM), F32),
        scratch_shapes=[pltpu.VMEM((nc, tile, tile), I32)] + _flash_scratch(HEAD_DIM, tile),
        compiler_params=_cparams(("parallel", "parallel")),
        name="dsa_prompt",
    )(p["iq3t"], p["wt"], p["ik3"], p["dqt"], p["dkb"], p["dvt"], d0, d1, far)


def _moba_prompt_kernel(q3_ref, qt_ref, k_ref, vt_ref, km_ref, d0_ref, d1_ref, far_ref, o_ref,
                        sel_ref, m_ref, l_ref, acc_ref):
    i = pl.program_id(1)
    tile = o_ref.shape[0]
    nblk = km_ref.shape[0]
    half = (N_HEADS // MOBA_KV_HEADS) * tile
    _flash_init(m_ref, l_ref, acc_ref)

    blk = lax.broadcasted_iota(I32, (nblk, tile), 0)
    valid = blk < i
    km = km_ref[...]
    for g in range(MOBA_KV_HEADS):
        km_hi, km_lo = _split_bf16(km[:, g * 64:(g + 1) * 64])
        for hd in range(g * 4, g * 4 + 4):
            cols = slice(hd * tile, (hd + 1) * tile)
            q_hi = q3_ref[0:64, cols]
            q_lo = q3_ref[128:192, cols]
            gs = _dot(km_hi, q_hi) + _dot(km_lo, q_hi) + _dot(km_hi, q_lo)
            cur = jnp.where(valid, gs, -jnp.inf)
            sel = jnp.full((nblk, tile), NEG, F32)
            for _ in range(MOBA_TOPK):
                mx = jnp.max(cur, axis=0, keepdims=True)
                first = jnp.min(jnp.where(cur == mx, blk, nblk), axis=0, keepdims=True)
                pick = (blk == first) & (mx > -jnp.inf)
                sel = jnp.where(pick, 0.0, sel)
                cur = jnp.where(pick, -jnp.inf, cur)
            sel_ref[:, cols] = sel

    def chunk(c, kind):
        rows = _chunk_rows(c, tile)
        vt_all = vt_ref[c]
        s_parts = [_dot(k_ref[g, rows, :], qt_ref[:, g * half:(g + 1) * half]) for g in range(MOBA_KV_HEADS)]
        vts = [vt_all[g * 64:(g + 1) * 64] for g in range(MOBA_KV_HEADS)]
        if kind == "diag":
            bias = dict(tile_bias=d0_ref)
        elif kind == "prev":
            bias = dict(tile_bias=d1_ref, col_off=sel_ref[pl.ds(c, 1), :])
        else:
            bias = dict(col_off=far_ref[...] + sel_ref[pl.ds(c, 1), :])
        _flash_update(s_parts, vts, tile, m_ref, l_ref, acc_ref, **bias)

    def body(c, carry):
        chunk(c, "far")
        return carry

    lax.fori_loop(0, jnp.maximum(i - 1, 0), body, 0)

    @pl.when(i >= 1)
    def _():
        chunk(i - 1, "prev")

    chunk(i, "diag")
    _flash_finish(o_ref, l_ref, acc_ref, tile)


def _moba_prompt(p, bias_tbl):
    b, t, _ = p["mk"].shape
    tile = ATT_TILE
    nc = t // tile
    d0, d1, far = _bias_tiles(bias_tbl, tile)
    km = p["km"].reshape(b, nc, 128)
    const = lambda a: pl.BlockSpec(a.shape, lambda bi, i: (0,) * a.ndim)
    return pl.pallas_call(
        _moba_prompt_kernel,
        grid=(b, nc),
        in_specs=[_wide_spec(256, tile), _wide_spec(HEAD_DIM, tile),
                  pl.BlockSpec((None, 2, t, 64), lambda bi, i: (bi, 0, 0, 0)),
                  pl.BlockSpec((None, nc, 128, tile), lambda bi, i: (bi, 0, 0, 0)),
                  pl.BlockSpec((None, nc, 128), lambda bi, i: (bi, 0, 0)),
                  const(d0), const(d1), const(far)],
        out_specs=pl.BlockSpec((None, tile, N_HEADS * HEAD_DIM), lambda bi, i: (bi, i, 0)),
        out_shape=jax.ShapeDtypeStruct((b, t, N_HEADS * HEAD_DIM), F32),
        scratch_shapes=[pltpu.VMEM((nc, N_HEADS * tile), F32)] + _flash_scratch(HEAD_DIM, tile),
        compiler_params=_cparams(("parallel", "parallel")),
        name="moba_prompt",
    )(p["mq3t"], p["mqt"], p["mkb"], p["mvt"], km, d0, d1, far)


PAGES_PER_STEP = 16


def _page_specs(page_shape, pages, n_chunks, layer, chunk_of):
    def spec(s):
        def index_map(b, j, pt):
            c = jnp.clip(chunk_of(j), 0, n_chunks - 1)
            return (layer, pt[b * (n_chunks * pages) + c * pages + s], 0, 0)
        return pl.BlockSpec((None, None) + page_shape, index_map)
    return [spec(s) for s in range(pages)]


def _key_minor(cache):
    return jnp.swapaxes(cache, 2, 3)


def _cat_pages(refs, axis=1):
    return jnp.concatenate([r[...] for r in refs], axis=axis)


def _row_flash_step(s, mask, vt, m_ref, l_ref, acc_ref, rows=slice(None), keys_major=False):
    if mask is not None:
        s = jnp.where(mask, s, NEG)
    m_old = m_ref[rows, :]
    m_new = jnp.maximum(m_old, jnp.max(s, axis=1, keepdims=True))
    p = jnp.exp(s - m_new)
    if mask is not None:
        p = jnp.where(mask, p, 0.0)
    alpha = jnp.exp(m_old - m_new)
    l_ref[rows, :] = l_ref[rows, :] * alpha + jnp.sum(p, axis=1, keepdims=True)
    pv = _dot(p.astype(BF16), vt) if keys_major else _dot_nt(p.astype(BF16), vt)
    acc_ref[rows, :] = acc_ref[rows, :] * alpha + pv
    m_ref[rows, :] = m_new


def _row_flash_last(s_new, include, v_new, m_ref, l_ref, acc_ref, o_ref):
    m_old = m_ref[...]
    m_new = jnp.where(include, jnp.maximum(m_old, s_new), m_old)
    p = jnp.where(include, jnp.exp(s_new - m_new), 0.0)
    alpha = jnp.exp(m_old - m_new)
    l = l_ref[...] * alpha + p
    o_ref[...] = (acc_ref[...] * alpha + p * v_new) / l


def _dsa_sample_kernel(k_sel, n_chunks, pages, *refs):
    pt_ref = refs[0]
    ik_pages = refs[1:1 + pages]
    k_pages = refs[1 + pages:1 + 2 * pages]
    v_pages = refs[1 + 2 * pages:1 + 3 * pages]
    (iq3_ref, w_ref, ikn_ref, dq_ref, dkn_ref, dvn_ref, bias_ref, bias0_ref, o_ref,
     key_ref, newkey_ref, t_ref, j_ref, m_ref, l_ref, acc_ref) = refs[1 + 3 * pages:]
    del pt_ref
    j = pl.program_id(1)
    span = pages * PAGE_SIZE
    q_hi = iq3_ref[:, 0:64]
    q_lo = iq3_ref[:, 128:192]
    lane = lax.broadcasted_iota(I32, (1, span), 1)

    @pl.when(j < n_chunks)
    def _():
        k_hi, k_lo = _split_bf16(_cat_pages(ik_pages))
        dots = _dot3(q_hi, q_lo, k_hi, k_lo, _dot)
        score = jnp.sum(jnp.maximum(dots, 0.0) * w_ref[...], axis=0, keepdims=True)
        key_ref[j] = _order_key(score)

    @pl.when(j == n_chunks - 1)
    def _():
        iq = q_hi.astype(F32) + q_lo.astype(F32)
        dots = jnp.sum(iq * ikn_ref[...], axis=1, keepdims=True)
        newkey = _order_key(jnp.sum(jnp.maximum(dots, 0.0) * w_ref[...], axis=0, keepdims=True))
        newkey_ref[...] = newkey
        new_idx = n_chunks * span

        def count(pred):
            tot = pred(newkey, new_idx).astype(I32)
            for c in range(n_chunks):
                tot = tot + jnp.sum(pred(key_ref[c], c * span + lane).astype(I32), axis=1, keepdims=True)
            return tot

        t, jj = _topk_threshold(
            lambda cand: count(lambda key, idx: key >= cand),
            lambda tt: (count(lambda key, idx: key > tt), count(lambda key, idx: key == tt)),
            lambda tt, cand: count(lambda key, idx: (key == tt) & (idx < cand)),
            k_sel, 1, max(1, new_idx.bit_length()))
        t_ref[...] = t
        j_ref[...] = jj
        _flash_init(m_ref, l_ref, acc_ref)

    @pl.when(j >= n_chunks)
    def _():
        c = j - n_chunks
        t, jj = t_ref[...], j_ref[...]
        key = key_ref[c]
        mask = (key > t) | ((key == t) & (c * span + lane <= jj))
        s = _dot(dq_ref[...], _cat_pages(k_pages).astype(BF16)) + bias_ref[...]
        _row_flash_step(s, mask, _cat_pages(v_pages).astype(BF16), m_ref, l_ref, acc_ref)

    @pl.when(j == 2 * n_chunks - 1)
    def _():
        t, jj = t_ref[...], j_ref[...]
        newkey = newkey_ref[...]
        include = (newkey > t) | ((newkey == t) & (n_chunks * span <= jj))
        s_new = jnp.sum(dq_ref[...].astype(F32) * dkn_ref[...], axis=1, keepdims=True) + bias0_ref[...]
        _row_flash_last(s_new, include, dvn_ref[...], m_ref, l_ref, acc_ref, o_ref)


def _sample_bias(bias_tbl, n_past):
    row = bias_tbl[_rel_bucket(n_past - jnp.arange(n_past))]
    return row.T, bias_tbl[0].reshape(-1, 1)


def _sample_common(page_table):
    db, n_pages = page_table.shape
    pages = min(PAGES_PER_STEP, n_pages)
    assert n_pages % pages == 0
    return db, n_pages, pages, n_pages // pages


def _dsa_sample(sp, caches, layer, page_table, bias_tbl):
    cache_ik, cache_k, cache_v = caches
    db, n_pages, pages, n_chunks = _sample_common(page_table)
    n_past = n_pages * PAGE_SIZE
    span = pages * PAGE_SIZE
    k_sel = min(DSA_TOPK, (n_past + 1) // 4)
    bias, bias0 = _sample_bias(bias_tbl, n_past)
    per_b = lambda *shape: pl.BlockSpec((None,) + shape, lambda b, j, pt: (b,) + (0,) * len(shape))
    const = lambda a: pl.BlockSpec(a.shape, lambda b, j, pt: (0,) * a.ndim)
    page = (64, PAGE_SIZE)
    in_specs = (_page_specs(page, pages, n_chunks, layer, lambda j: j)
                + _page_specs(page, pages, n_chunks, layer, lambda j: j - n_chunks)
                + _page_specs(page, pages, n_chunks, layer, lambda j: j - n_chunks)
                + [per_b(N_HEADS, 256), per_b(N_HEADS, 1), per_b(1, 64), per_b(N_HEADS, 64), per_b(1, 64), per_b(1, 64),
                   pl.BlockSpec((N_HEADS, span), lambda b, j, pt: (0, jnp.clip(j - n_chunks, 0, n_chunks - 1))),
                   const(bias0)])
    grid_spec = pltpu.PrefetchScalarGridSpec(
        num_scalar_prefetch=1, grid=(db, 2 * n_chunks), in_specs=in_specs,
        out_specs=per_b(N_HEADS, HEAD_DIM),
        scratch_shapes=[pltpu.VMEM((n_chunks, 1, span), I32), pltpu.VMEM((1, 1), I32), pltpu.VMEM((1, 1), I32),
                        pltpu.VMEM((1, 1), I32), pltpu.VMEM((N_HEADS, 1), F32), pltpu.VMEM((N_HEADS, 1), F32),
                        pltpu.VMEM((N_HEADS, HEAD_DIM), F32)])
    return pl.pallas_call(
        functools.partial(_dsa_sample_kernel, k_sel, n_chunks, pages),
        grid_spec=grid_spec,
        out_shape=jax.ShapeDtypeStruct((db, N_HEADS, HEAD_DIM), F32),
        compiler_params=_cparams(("parallel", "arbitrary")),
        name="dsa_sample",
    )(page_table.reshape(-1), *([cache_ik] * pages), *([cache_k] * pages), *([cache_v] * pages),
      sp["iq3"], sp["w"], sp["ik"], sp["dq"], sp["dk"], sp["dv"], bias, bias0)


def _moba_sample_kernel(n_chunks, pages, *refs):
    k1_pages = refs[1:1 + pages]
    k2_pages = refs[1 + pages:1 + 2 * pages]
    v_pages = refs[1 + 2 * pages:1 + 3 * pages]
    (q3_ref, q_ref, kn_ref, vn_ref, bias_ref, bias0_ref, o_ref,
     gs_ref, sel_ref, m_ref, l_ref, acc_ref) = refs[1 + 3 * pages:]
    j = pl.program_id(1)
    span = pages * PAGE_SIZE
    blocks = span // MOBA_BLOCK
    lanes = gs_ref.shape[-1]

    @pl.when(j < n_chunks)
    def _():
        kt = _cat_pages(k1_pages)
        lane = lax.broadcasted_iota(I32, (kt.shape[0], lanes), 1)
        kmt = jnp.zeros((kt.shape[0], lanes), F32)
        for n in range(blocks):
            mean = jnp.sum(kt[:, n * MOBA_BLOCK:(n + 1) * MOBA_BLOCK], axis=1, keepdims=True) * (1.0 / MOBA_BLOCK)
            kmt = jnp.where(lane == n, mean, kmt)
        for g in range(MOBA_KV_HEADS):
            rows = slice(g * 4, (g + 1) * 4)
            km_hi, km_lo = _split_bf16(kmt[g * 64:(g + 1) * 64])
            gs_ref[j, rows, :] = _dot3(q3_ref[rows, 0:64], q3_ref[rows, 128:192], km_hi, km_lo, _dot)

    @pl.when(j == n_chunks - 1)
    def _():
        shape = gs_ref.shape
        lane = lax.broadcasted_iota(I32, shape, 2)
        blk = lax.broadcasted_iota(I32, shape, 0) * blocks + lane
        reduce = lambda f, a: f(f(a, axis=2, keepdims=True), axis=0, keepdims=True)
        cur = jnp.where(lane < blocks, gs_ref[...], -jnp.inf)
        sel = jnp.zeros(shape, F32)
        for _ in range(min(MOBA_TOPK, n_chunks * blocks)):
            mx = reduce(jnp.max, cur)
            first = reduce(jnp.min, jnp.where(cur == mx, blk, n_chunks * blocks))
            pick = (blk == first) & (mx > -jnp.inf)
            sel = jnp.where(pick, 1.0, sel)
            cur = jnp.where(pick, -jnp.inf, cur)
        sel_ref[...] = sel
        _flash_init(m_ref, l_ref, acc_ref)

    @pl.when(j >= n_chunks)
    def _():
        c = j - n_chunks
        kt = _cat_pages(k2_pages).astype(BF16)
        vt = _cat_pages(v_pages).astype(BF16)
        expand = (lax.broadcasted_iota(I32, (lanes, span), 1) // MOBA_BLOCK
                  == lax.broadcasted_iota(I32, (lanes, span), 0)).astype(F32)
        mask_all = _dot(sel_ref[c], expand) > 0.5
        for g in range(MOBA_KV_HEADS):
            rows = slice(g * 4, (g + 1) * 4)
            feat = slice(g * 64, (g + 1) * 64)
            s = _dot(q_ref[rows, :], kt[feat]) + bias_ref[rows, :]
            _row_flash_step(s, mask_all[rows], vt[feat], m_ref, l_ref, acc_ref, rows)

    @pl.when(j == 2 * n_chunks - 1)
    def _():
        q = q_ref[...].astype(F32)
        kn = kn_ref[...]
        vn = vn_ref[...]
        grp = lax.broadcasted_iota(I32, (N_HEADS, 1), 0) >= 4
        s_new = jnp.where(grp, jnp.sum(q * kn[:, 64:], axis=1, keepdims=True),
                          jnp.sum(q * kn[:, :64], axis=1, keepdims=True)) + bias0_ref[...]
        v_new = jnp.where(grp, vn[:, 64:], vn[:, :64])
        _row_flash_last(s_new, True, v_new, m_ref, l_ref, acc_ref, o_ref)


def _moba_sample(sp, caches, layer, page_table, bias_tbl):
    cache_k, cache_v = caches
    db, n_pages, pages, n_chunks = _sample_common(page_table)
    n_past = n_pages * PAGE_SIZE
    span = pages * PAGE_SIZE
    assert span % MOBA_BLOCK == 0
    blocks = span // MOBA_BLOCK
    bias, bias0 = _sample_bias(bias_tbl, n_past)
    per_b = lambda *shape: pl.BlockSpec((None,) + shape, lambda b, j, pt: (b,) + (0,) * len(shape))
    const = lambda a: pl.BlockSpec(a.shape, lambda b, j, pt: (0,) * a.ndim)
    page = (MOBA_KV_HEADS * HEAD_DIM, PAGE_SIZE)
    in_specs = (_page_specs(page, pages, n_chunks, layer, lambda j: j)
                + _page_specs(page, pages, n_chunks, layer, lambda j: j - n_chunks)
                + _page_specs(page, pages, n_chunks, layer, lambda j: j - n_chunks)
                + [per_b(N_HEADS, 256), per_b(N_HEADS, 64), per_b(1, 128), per_b(1, 128),
                   pl.BlockSpec((N_HEADS, span), lambda b, j, pt: (0, jnp.clip(j - n_chunks, 0, n_chunks - 1))),
                   const(bias0)])
    assert blocks <= 128
    grid_spec = pltpu.PrefetchScalarGridSpec(
        num_scalar_prefetch=1, grid=(db, 2 * n_chunks), in_specs=in_specs,
        out_specs=per_b(N_HEADS, HEAD_DIM),
        scratch_shapes=[pltpu.VMEM((n_chunks, N_HEADS, 128), F32), pltpu.VMEM((n_chunks, N_HEADS, 128), F32),
                        pltpu.VMEM((N_HEADS, 1), F32), pltpu.VMEM((N_HEADS, 1), F32),
                        pltpu.VMEM((N_HEADS, HEAD_DIM), F32)])
    return pl.pallas_call(
        functools.partial(_moba_sample_kernel, n_chunks, pages),
        grid_spec=grid_spec,
        out_shape=jax.ShapeDtypeStruct((db, N_HEADS, HEAD_DIM), F32),
        compiler_params=_cparams(("parallel", "arbitrary")),
        name="moba_sample",
    )(page_table.reshape(-1), *([cache_k] * (2 * pages)), *([cache_v] * pages),
      sp["mq3"], sp["mq"], sp["mk"], sp["mv"], bias, bias0)


def _mla_sample_kernel(n_chunks, pages, *refs):
    c_pages = refs[1:1 + pages]
    r_pages = refs[1 + pages:1 + 2 * pages]
    q_ref, cn_ref, rn_ref, o_ref, m_ref, l_ref, acc_ref = refs[1 + 2 * pages:]
    j = pl.program_id(1)
    q_lat = q_ref[:, 0:MLA_KV_LORA]
    q_rope = q_ref[:, MLA_KV_LORA:MLA_QK]

    @pl.when(j == 0)
    def _():
        _flash_init(m_ref, l_ref, acc_ref)

    ckv = _cat_pages(c_pages, axis=0).astype(BF16)
    s = _dot_nt(q_lat, ckv) + _dot(q_rope, _cat_pages(r_pages).astype(BF16))
    _row_flash_step(s, None, ckv, m_ref, l_ref, acc_ref, keys_major=True)

    @pl.when(j == n_chunks - 1)
    def _():
        s_new = (jnp.sum(q_lat.astype(F32) * cn_ref[...], axis=1, keepdims=True)
                 + jnp.sum(q_rope.astype(F32) * rn_ref[...], axis=1, keepdims=True))
        _row_flash_last(s_new, True, cn_ref[...], m_ref, l_ref, acc_ref, o_ref)


def _mla_sample(sp, caches, layer, page_table):
    cache_c, cache_r = caches
    db, n_pages, pages, n_chunks = _sample_common(page_table)
    per_b = lambda *shape: pl.BlockSpec((None,) + shape, lambda b, j, pt: (b,) + (0,) * len(shape))
    in_specs = (_page_specs((PAGE_SIZE, MLA_KV_LORA), pages, n_chunks, layer, lambda j: j)
                + _page_specs((MLA_ROPE, PAGE_SIZE), pages, n_chunks, layer, lambda j: j)
                + [per_b(N_HEADS, MLA_QK_PAD), per_b(1, MLA_KV_LORA), per_b(1, MLA_ROPE)])
    grid_spec = pltpu.PrefetchScalarGridSpec(
        num_scalar_prefetch=1, grid=(db, n_chunks), in_specs=in_specs,
        out_specs=per_b(N_HEADS, MLA_KV_LORA),
        scratch_shapes=[pltpu.VMEM((N_HEADS, 1), F32), pltpu.VMEM((N_HEADS, 1), F32),
                        pltpu.VMEM((N_HEADS, MLA_KV_LORA), F32)])
    return pl.pallas_call(
        functools.partial(_mla_sample_kernel, n_chunks, pages),
        grid_spec=grid_spec,
        out_shape=jax.ShapeDtypeStruct((db, N_HEADS, MLA_KV_LORA), F32),
        compiler_params=_cparams(("parallel", "arbitrary")),
        name="mla_sample",
    )(page_table.reshape(-1), *([cache_c] * pages), *([cache_r] * pages), sp["q"], sp["ckv"], sp["kr"])


def _layer_norm(y, g, b):
    mu = jnp.mean(y, axis=-1, keepdims=True)
    d = y - mu
    var = jnp.mean(d * d, axis=-1, keepdims=True)
    return d * lax.rsqrt(var + LN_EPS) * g + b


def _merge_kernel(alpha, x_ref, oa_ref, ob_ref, ol_ref, wg_ref, wuv_ref, wb_ref, wo_ref, g_ref, b_ref, y_ref):
    x = x_ref[...]
    gates = jax.nn.sigmoid(_dot(x.astype(BF16), wg_ref[...]))
    oc = _dot(ol_ref[...].astype(BF16), wuv_ref[...])
    mixed = jnp.zeros_like(x)
    for n, o in enumerate((oa_ref[...], ob_ref[...], oc)):
        mixed = mixed + gates[:, n * D_MODEL:(n + 1) * D_MODEL] * _dot(o.astype(BF16), wb_ref[n])
    y = alpha * x + _dot(mixed.astype(BF16), wo_ref[...])
    y_ref[...] = _layer_norm(y, g_ref[...], b_ref[...])


def _merge(x, o_a, o_b, o_lat, mw, alpha, tm):
    n = x.shape[0]
    w_gate, wuv_bd, w_branch, w_out, ln_g, ln_b = mw
    tok = lambda w: pl.BlockSpec((tm, w), lambda i: (i, 0))
    const = lambda a: pl.BlockSpec(a.shape, lambda i: (0,) * a.ndim)
    consts = (w_gate, wuv_bd, w_branch, w_out, ln_g, ln_b)
    return pl.pallas_call(
        functools.partial(_merge_kernel, alpha),
        grid=(n // tm,),
        in_specs=[tok(D_MODEL), tok(BRANCH_WIDTH), tok(BRANCH_WIDTH), tok(N_HEADS * MLA_KV_LORA)]
        + [const(a) for a in consts],
        out_specs=tok(D_MODEL),
        out_shape=jax.ShapeDtypeStruct((n, D_MODEL), F32),
        compiler_params=_cparams(("parallel",)),
        name="merge",
    )(x, o_a, o_b, o_lat, *consts)


def _prep_merge_weights(w_in, w_uv, w_branch, w_out, ln_g, ln_b):
    w_gate = w_in[:, IN_OFFS[12]:IN_OFFS[13]].astype(BF16)
    eye = jnp.eye(N_HEADS, dtype=F32)
    wuv_bd = (eye[:, None, :, None] * w_uv[:, :, None, :]).reshape(N_HEADS * MLA_KV_LORA, BRANCH_WIDTH)
    return (w_gate, wuv_bd.astype(BF16), w_branch.astype(BF16), w_out.astype(BF16),
            ln_g.reshape(1, -1), ln_b.reshape(1, -1))


def _extract_top(cur, count, out_ref):
    rows = cur.shape[0]
    iota = lax.broadcasted_iota(I32, cur.shape, 0)
    for r in range(count):
        mx = jnp.max(cur, axis=0, keepdims=True)
        first = jnp.min(jnp.where(cur == mx, iota, rows), axis=0, keepdims=True)
        cur = jnp.where(iota == first, -jnp.inf, cur)
        out_ref[r:r + 1, :] = mx


def _peer_select_kernel(x_ref, wq_hi_ref, wq_lo_ref, k_hi_ref, k_lo_ref,
                        s_ref, e_ref, t_ref, v0_ref, v1_ref, cand_ref, best_ref):
    x_hi, x_lo = _split_bf16(x_ref[...])
    qt = _dot3(wq_hi_ref[...], wq_lo_ref[...], x_hi, x_lo, _dot_nt)
    half = PEER_KEY_DIM // 2
    for hd in range(PEER_HEADS):
        for p, v_ref in ((0, v0_ref), (1, v1_ref)):
            r0 = (hd * 2 + p) * half
            q_hi, q_lo = _split_bf16(qt[r0:r0 + half])
            s = _dot3(k_hi_ref[p, hd], k_lo_ref[p, hd], q_hi, q_lo, _dot)
            s_ref[p, hd] = s
            _extract_top(s, PEER_TOPK, v_ref)
        v1 = v1_ref[...]
        for a in range(PEER_TOPK):
            cand_ref[a * PEER_TOPK:(a + 1) * PEER_TOPK, :] = v0_ref[a:a + 1, :] + v1
        _extract_top(cand_ref[...], PEER_TOPK, best_ref)
        best = best_ref[...]
        top = best[0:1, :]
        z = jnp.sum(jnp.exp(best - top), axis=0, keepdims=True)
        t_ref[hd:hd + 1, :] = best[PEER_TOPK - 1:PEER_TOPK, :]
        e_ref[0, hd] = jnp.exp(s_ref[0, hd] - v0_ref[0:1, :])
        e_ref[1, hd] = jnp.exp(s_ref[1, hd] - v1_ref[0:1, :]) / z


def _peer_select(x, pw, tm):
    n = x.shape[0]
    wq_hi, wq_lo, k_hi, k_lo = pw[:4]
    const = lambda a: pl.BlockSpec(a.shape, lambda i: (0,) * a.ndim)
    sblock = pl.BlockSpec((2, PEER_HEADS, PEER_NKEYS, tm), lambda i: (0, 0, 0, i))
    sshape = jax.ShapeDtypeStruct((2, PEER_HEADS, PEER_NKEYS, n), F32)
    return pl.pallas_call(
        _peer_select_kernel,
        grid=(n // tm,),
        in_specs=[pl.BlockSpec((tm, D_MODEL), lambda i: (i, 0))] + [const(a) for a in (wq_hi, wq_lo, k_hi, k_lo)],
        out_specs=[sblock, sblock, pl.BlockSpec((PEER_HEADS, tm), lambda i: (0, i))],
        out_shape=[sshape, sshape, jax.ShapeDtypeStruct((PEER_HEADS, n), F32)],
        scratch_shapes=[pltpu.VMEM((PEER_TOPK, tm), F32), pltpu.VMEM((PEER_TOPK, tm), F32),
                        pltpu.VMEM((PEER_TOPK * PEER_TOPK, tm), F32), pltpu.VMEM((PEER_TOPK, tm), F32)],
        compiler_params=_cparams(("parallel",)),
        name="peer_select",
    )(x, wq_hi, wq_lo, k_hi, k_lo)


def _gelu_tanh(x):
    return 0.5 * x * (1.0 + jnp.tanh(math.sqrt(2.0 / math.pi) * (x + 0.044715 * (x * x * x))))


PEER_LANE_BLOCK = 128


def _peer_dense_kernel(alpha, x_ref, s_ref, e_ref, t_ref, u_ref, vt_ref, g_ref, b_ref, y_ref,
                       acc_ref, act_ref, y_scr):
    c = pl.program_id(1)
    rows_per_step = u_ref.shape[0] // PEER_NKEYS

    @pl.when(c == 0)
    def _():
        acc_ref[...] = jnp.zeros(acc_ref.shape, F32)

    act_ref[...] = _gelu_tanh(_dot_nt(u_ref[...], x_ref[...].astype(BF16)))
    tm = act_ref.shape[1]
    lb = min(PEER_LANE_BLOCK, tm)
    assert rows_per_step == 8
    first = pl.ds(pl.multiple_of(c * rows_per_step, rows_per_step), rows_per_step)
    for tb in range(tm // lb):
        cols = slice(tb * lb, (tb + 1) * lb)
        for ii in range(rows_per_step):
            rows = slice(ii * PEER_NKEYS, (ii + 1) * PEER_NKEYS)
            w = jnp.zeros((PEER_NKEYS, lb), F32)
            for hd in range(PEER_HEADS):
                s0 = s_ref[0, hd, first, cols][ii:ii + 1]
                e0 = e_ref[0, hd, first, cols][ii:ii + 1]
                pair = s0 + s_ref[1, hd, :, cols]
                w = w + jnp.where(pair >= t_ref[hd:hd + 1, cols], e0 * e_ref[1, hd, :, cols], 0.0)
            y_scr[rows, cols] = (act_ref[rows, cols] * w).astype(BF16)
    acc_ref[...] += _dot(vt_ref[...], y_scr[...])

    @pl.when(c == pl.num_programs(1) - 1)
    def _():
        y_ref[...] = _layer_norm(alpha * x_ref[...] + acc_ref[...].T, g_ref[...], b_ref[...])


def _peer_dense(x, s, e, t, pw, alpha, tm, ec):
    n = x.shape[0]
    u_bf, vt_bf, ln_g, ln_b = pw[4:]
    n_exp = u_bf.shape[0]
    sblock = pl.BlockSpec((2, PEER_HEADS, PEER_NKEYS, tm), lambda i, c: (0, 0, 0, i))
    const = lambda a: pl.BlockSpec(a.shape, lambda i, c: (0,) * a.ndim)
    return pl.pallas_call(
        functools.partial(_peer_dense_kernel, alpha),
        grid=(n // tm, n_exp // ec),
        in_specs=[pl.BlockSpec((tm, D_MODEL), lambda i, c: (i, 0)), sblock, sblock,
                  pl.BlockSpec((PEER_HEADS, tm), lambda i, c: (0, i)),
                  pl.BlockSpec((ec, D_MODEL), lambda i, c: (c, 0)),
                  pl.BlockSpec((D_MODEL, ec), lambda i, c: (0, c)),
                  const(ln_g), const(ln_b)],
        out_specs=pl.BlockSpec((tm, D_MODEL), lambda i, c: (i, 0)),
        out_shape=jax.ShapeDtypeStruct((n, D_MODEL), F32),
        scratch_shapes=[pltpu.VMEM((D_MODEL, tm), F32), pltpu.VMEM((ec, tm), F32), pltpu.VMEM((ec, tm), BF16)],
        compiler_params=_cparams(("parallel", "arbitrary")),
        name="peer_dense",
    )(x, s, e, t, u_bf, vt_bf, ln_g, ln_b)


def _prep_peer_weights(peer_wq, peer_keys, peer_u, peer_v, ln_g, ln_b):
    wq_hi, wq_lo = _split_bf16(peer_wq.T)
    k_hi, k_lo = _split_bf16(peer_keys)
    return (wq_hi, wq_lo, k_hi, k_lo, peer_u.astype(BF16), peer_v.T.astype(BF16),
            ln_g.reshape(1, -1), ln_b.reshape(1, -1))


PROMPT_TILE = ATT_TILE
PEER_SELECT_TILE = 256
PEER_TOKEN_TILE = 512
PEER_EXPERT_CHUNK = 8 * PEER_NKEYS


def _channel_mix(x1, peer_w, alpha):
    n = x1.shape[0]
    s, e, t = _peer_select(x1, peer_w, min(PEER_SELECT_TILE, n))
    return _peer_dense(x1, s, e, t, peer_w, alpha, min(PEER_TOKEN_TILE, n), PEER_EXPERT_CHUNK)


def _sample_views(q):
    db = q["dk"].shape[1]
    heads_major = lambda a: jnp.transpose(a.reshape(a.shape[2], N_HEADS, db), (2, 1, 0))
    rows = lambda a: a[0][:, None, :]
    return {
        "iq3": heads_major(q["iq3t"]), "w": heads_major(q["wt"]), "ik": rows(q["ik"]),
        "dq": heads_major(q["dqt"]), "dk": rows(q["dk"]), "dv": rows(q["dv"]),
        "mq3": heads_major(q["mq3t"]), "mq": heads_major(q["mqt"]),
        "mk": rows(q["mk"]), "mv": rows(q["mv"]),
        "q": heads_major(q["qt"]), "ckv": rows(q["ckv"]), "kr": rows(q["kr"]),
    }


def kernel(x_prompt, x_sample, cache_dsa_k, cache_dsa_v, cache_dsa_idx_k, cache_moba_k, cache_moba_v, cache_mla_ckv, cache_mla_krope, page_table, rel_bias, w_in, w_uq, g_cq, g_ckv, w_uk, w_uv, w_branch, w_out, ln1_g, ln1_b, peer_wq, peer_keys, peer_u, peer_v, ln2_g, ln2_b):
    b, t, _ = x_prompt.shape
    db, dec_seq, _ = x_sample.shape
    assert dec_seq == 1 and t % ATT_TILE == 0
    depth = w_in.shape[0]
    n_pages = page_table.shape[1]
    n_past = n_pages * PAGE_SIZE
    assert n_past % MOBA_BLOCK == 0
    alpha = (2 * depth) ** 0.25
    pos_p = jnp.broadcast_to(jnp.arange(t, dtype=F32), (b, t))
    pos_s = jnp.full((1, db), n_past, F32)
    moba_pages = lambda c: jnp.transpose(c, (0, 1, 3, 4, 2)).reshape(
        c.shape[:2] + (MOBA_KV_HEADS * HEAD_DIM, PAGE_SIZE))
    dsa_caches = tuple(_key_minor(c) for c in (cache_dsa_idx_k, cache_dsa_k, cache_dsa_v))
    moba_caches = (moba_pages(cache_moba_k), moba_pages(cache_moba_v))
    mla_caches = (cache_mla_ckv, _key_minor(cache_mla_krope))
    bias_a, bias_b = rel_bias[:, :N_HEADS], rel_bias[:, N_HEADS:]

    hp = x_prompt.reshape(b * t, D_MODEL)
    hs = x_sample.reshape(db, D_MODEL)
    rows_p, rows_s = [], []
    for l in range(depth):
        proj_w = _prep_proj_weights(w_in[l], w_uq[l], w_uk[l])
        merge_w = _prep_merge_weights(w_in[l], w_uv[l], w_branch[l], w_out[l], ln1_g[l], ln1_b[l])
        peer_w = _prep_peer_weights(peer_wq[l], peer_keys[l], peer_u[l], peer_v[l], ln2_g[l], ln2_b[l])

        p = _project(hp.reshape(b, t, D_MODEL), pos_p, proj_w, g_cq[l], g_ckv[l], PROMPT_TILE)
        flat = lambda a: a.reshape(b * t, a.shape[-1])
        x1 = _merge(hp, flat(_dsa_prompt(p, bias_a)), flat(_moba_prompt(p, bias_b)), flat(_mla_prompt(p)),
                    merge_w, alpha, PROMPT_TILE)
        hp = _channel_mix(x1, peer_w, alpha)
        rows_p.append(p)

        q = _project(hs.reshape(1, db, D_MODEL), pos_s, proj_w, g_cq[l], g_ckv[l], db)
        sp = _sample_views(q)
        o_a = _dsa_sample(sp, dsa_caches, l, page_table, bias_a)
        o_b = _moba_sample(sp, moba_caches, l, page_table, bias_b)
        o_l = _mla_sample(sp, mla_caches, l, page_table)
        x1 = _merge(hs, o_a.reshape(db, -1), o_b.reshape(db, -1), o_l.reshape(db, -1), merge_w, alpha, db)
        hs = _channel_mix(x1, peer_w, alpha)
        rows_s.append(q)

    def stack(name, tail):
        new_p = jnp.stack([r[name].reshape((b, t) + tail) for r in rows_p])
        new_s = jnp.stack([r[name].reshape((db, 1) + tail) for r in rows_s])
        return new_p, new_s

    outs = [hp.reshape(b, t, D_MODEL), hs.reshape(db, 1, D_MODEL)]
    for name, tail in (("dk", (HEAD_DIM,)), ("dv", (HEAD_DIM,)), ("ik", (DSA_IDX_DIM,)),
                       ("mk", (MOBA_KV_HEADS, HEAD_DIM)), ("mv", (MOBA_KV_HEADS, HEAD_DIM)),
                       ("ckv", (MLA_KV_LORA,)), ("kr", (MLA_ROPE,))):
        outs.extend(stack(name, tail))
    return tuple(outs)
```
